```python
import functools
import jax, jax.numpy as jnp
from jax import lax
import numpy as np

D_MODEL = 2048
BATCH = 1
SEQ = 8192
DEPTH = 2
DEC_BATCH = 128
DEC_SEQ = 8
PAST_LEN = 2048
PAGE_SIZE = 128

N_HEADS = 16
N_KV_HEADS = 4
HEAD_DIM = 128
ATTN_DIM = N_HEADS * HEAD_DIM
KV_DIM = N_KV_HEADS * HEAD_DIM
IDX_HEADS = 16
IDX_DIM = 64
TOPK_MAX = 256
CONV_DIM = D_MODEL // 2
CONV_WIDTH = 3
D_FF = -(-8 * D_MODEL // (3 * 256)) * 256
ROPE_THETA = 10000.0
QUERY_BLOCK = 128
NORM_EPS = 1e-6
SPLIT_SIZES = (ATTN_DIM, KV_DIM, KV_DIM, IDX_HEADS * IDX_DIM, IDX_DIM, IDX_HEADS,
               CONV_DIM, CONV_DIM, CONV_DIM, D_MODEL, D_MODEL)
IN_DIM = sum(SPLIT_SIZES)

kernel_name = 'hybrid_dsa_shortconv_decode_step'


def rmsnorm(x, g):
    xf = x.astype(jnp.float32)
    y = xf * lax.rsqrt(jnp.mean(xf * xf, axis=-1, keepdims=True) + NORM_EPS)
    return (y * g.astype(jnp.float32)).astype(x.dtype)


def rope(x, pos):
    d = x.shape[-1]
    inv = ROPE_THETA ** (-jnp.arange(0, d, 2, dtype=jnp.float32) / d)
    ang = pos.astype(jnp.float32)[:, None] * inv[None, :]
    cos = jnp.cos(ang)[:, None, :]
    sin = jnp.sin(ang)[:, None, :]
    xf = x.astype(jnp.float32)
    x1, x2 = xf[..., : d // 2], xf[..., d // 2:]
    return jnp.concatenate([x1 * cos - x2 * sin, x2 * cos + x1 * sin], axis=-1).astype(x.dtype)


def split_in(z):
    out, off = [], 0
    for n in SPLIT_SIZES:
        out.append(z[..., off:off + n])
        off += n
    return out


def indexer_scores(qi, wi, ki):
    dots = jnp.einsum('...qhd,...ld->...qhl', qi.astype(jnp.float32), ki.astype(jnp.float32))
    w = wi.astype(jnp.float32) * (IDX_HEADS ** -0.5 * IDX_DIM ** -0.5)
    return jnp.einsum('...qhl,...qh->...ql', jax.nn.relu(dots), w)


def attend_selected(q, k_sel, v_sel, valid):
    g = N_HEADS // N_KV_HEADS
    qg = q.reshape(q.shape[:-2] + (N_KV_HEADS, g, HEAD_DIM)).astype(jnp.float32)
    s = jnp.einsum('...qgjd,...qkgd->...qgjk', qg, k_sel.astype(jnp.float32)) * (HEAD_DIM ** -0.5)
    s = jnp.where(valid[..., :, None, None, :], s, -jnp.inf)
    p = jax.nn.softmax(s, axis=-1)
    o = jnp.einsum('...qgjk,...qkgd->...qgjd', p, v_sel.astype(jnp.float32))
    return o.reshape(q.shape[:-2] + (ATTN_DIM,)).astype(q.dtype)


def prompt_sparse_attention(q, k, v, qi, ki, wi):
    b, s = q.shape[:2]
    topk = min(TOPK_MAX, s // 4)
    nb = s // QUERY_BLOCK
    key_pos = jnp.arange(s)
    bidx = jnp.arange(b)[:, None, None]

    def blocks(a):
        return jnp.moveaxis(a.reshape((b, nb, QUERY_BLOCK) + a.shape[2:]), 1, 0)

    def one_block(args):
        qb, qib, wib, blk = args
        qpos = blk * QUERY_BLOCK + jnp.arange(QUERY_BLOCK)
        sc = indexer_scores(qib, wib, ki)
        sc = jnp.where((key_pos[None, :] <= qpos[:, None])[None], sc, -jnp.inf)
        _, idx = lax.top_k(sc, topk)
        valid = idx <= qpos[None, :, None]
        return attend_selected(qb, k[bidx, idx], v[bidx, idx], valid)

    out = lax.map(one_block, (blocks(q), blocks(qi), blocks(wi), jnp.arange(nb)))
    return jnp.moveaxis(out, 0, 1).reshape(b, s, ATTN_DIM)


def sample_sparse_attention(q, k_new, v_new, qi, ki_new, wi, *, cache_k, cache_v, cache_ki, page_table):
    t = q.shape[1]
    total = PAST_LEN + t
    topk = min(TOPK_MAX, total // 4)
    qpos = PAST_LEN + jnp.arange(t)
    key_pos = jnp.arange(total)

    def one_seq(args):
        qs, ks, vs, qis, kis, wis, pt = args
        ki_past = cache_ki[pt].reshape(PAST_LEN, IDX_DIM).astype(kis.dtype)
        ki_all = jnp.concatenate([ki_past, kis], axis=0)
        sc = indexer_scores(qis, wis, ki_all)
        sc = jnp.where(key_pos[None, :] <= qpos[:, None], sc, -jnp.inf)
        _, idx = lax.top_k(sc, topk)
        in_past = (idx < PAST_LEN)[..., None, None]
        pidx = jnp.minimum(idx, PAST_LEN - 1)
        phys = pt[pidx // PAGE_SIZE]
        row = pidx % PAGE_SIZE
        nidx = jnp.clip(idx - PAST_LEN, 0, t - 1)
        k_sel = jnp.where(in_past, cache_k[phys, row].astype(ks.dtype), ks[nidx])
        v_sel = jnp.where(in_past, cache_v[phys, row].astype(vs.dtype), vs[nidx])
        valid = idx <= qpos[:, None]
        return attend_selected(qs, k_sel, v_sel, valid)

    return lax.map(one_seq, (q, k_new, v_new, qi, ki_new, wi, page_table))


def short_conv(u, prefix, w):
    t = u.shape[1]
    up = jnp.concatenate([prefix.astype(u.dtype), u], axis=1)
    y = sum(w[j] * up[:, j:j + t] for j in range(CONV_WIDTH))
    return y, up[:, -(CONV_WIDTH - 1):]


def mixer(xn, pos, w_in, w_conv, w_attn_out, w_conv_out, w_o, attn_fn, conv_prefix):
    n, t = xn.shape[:2]
    q, k, v, qi, ki, wi, cb, cc, ch, ga, gb = split_in(xn @ w_in)
    q = rope(q.reshape(n, t, N_HEADS, HEAD_DIM), pos)
    k = rope(k.reshape(n, t, N_KV_HEADS, HEAD_DIM), pos)
    v = v.reshape(n, t, N_KV_HEADS, HEAD_DIM)
    qi = rope(qi.reshape(n, t, IDX_HEADS, IDX_DIM), pos)
    ki = rope(ki.reshape(n, t, 1, IDX_DIM), pos)[:, :, 0]
    a = attn_fn(q, k, v, qi, ki, wi) @ w_attn_out
    c_conv, conv_state = short_conv(cc * ch, conv_prefix, w_conv)
    c = (cb * c_conv) @ w_conv_out
    merged = jax.nn.sigmoid(ga) * a + jax.nn.sigmoid(gb) * c
    return merged @ w_o, (k, v, ki, conv_state)


def swiglu(x, w_in, w_out):
    gu = x @ w_in
    return (jax.nn.silu(gu[..., :D_FF]) * gu[..., D_FF:]) @ w_out


def run_trunk(x, pos, make_attn, conv_prefix, norm_mix, w_in, w_conv, w_attn_out, w_conv_out, w_o,
              norm_ffn, w_ffn_in, w_ffn_out, norm_final):
    ks, vs, kis, convs = [], [], [], []
    for l in range(DEPTH):
        h, (k, v, ki, cs) = mixer(rmsnorm(x, norm_mix[l]), pos, w_in[l], w_conv[l], w_attn_out[l],
                                   w_conv_out[l], w_o[l], make_attn(l), conv_prefix[l])
        x = x + h
        x = x + swiglu(rmsnorm(x, norm_ffn[l]), w_ffn_in[l], w_ffn_out[l])
        ks.append(k)
        vs.append(v)
        kis.append(ki)
        convs.append(cs)
    return rmsnorm(x, norm_final), jnp.stack(ks), jnp.stack(vs), jnp.stack(kis), jnp.stack(convs)


def setup_inputs(seed: int = 0) -> dict:
    key = jax.random.key(seed)
    ks = jax.random.split(key, 20)
    n_pages = PAST_LEN // PAGE_SIZE
    n_used = DEC_BATCH * n_pages
    n_phys = n_used + max(1, n_used // 4)

    def w(k, shape, fan_in):
        return jax.random.normal(k, shape, jnp.float32) * (fan_in ** -0.5)

    def gain(k, shape):
        return 1.0 + 0.01 * jax.random.normal(k, shape, jnp.float32)

    page_table = jax.random.permutation(ks[0], n_phys)[:n_used].reshape(DEC_BATCH, n_pages).astype(jnp.int32)
    return {
        'x_prompt': jax.random.normal(ks[1], (BATCH, SEQ, D_MODEL), jnp.float32),
        'x_sample': jax.random.normal(ks[2], (DEC_BATCH, DEC_SEQ, D_MODEL), jnp.float32),
        'cache_k': jax.random.normal(ks[3], (DEPTH, n_phys, PAGE_SIZE, N_KV_HEADS, HEAD_DIM), jnp.float32),
        'cache_v': jax.random.normal(ks[4], (DEPTH, n_phys, PAGE_SIZE, N_KV_HEADS, HEAD_DIM), jnp.float32),
        'cache_ki': jax.random.normal(ks[5], (DEPTH, n_phys, PAGE_SIZE, IDX_DIM), jnp.float32),
        'state_conv': jax.random.normal(ks[6], (DEPTH, DEC_BATCH, CONV_WIDTH - 1, CONV_DIM), jnp.float32),
        'page_table': page_table,
        'norm_mix': gain(ks[7], (DEPTH, D_MODEL)),
        'w_in': w(ks[8], (DEPTH, D_MODEL, IN_DIM), D_MODEL),
        'w_conv': w(ks[9], (DEPTH, CONV_WIDTH, CONV_DIM), CONV_WIDTH),
        'w_attn_out': w(ks[10], (DEPTH, ATTN_DIM, D_MODEL), ATTN_DIM),
        'w_conv_out': w(ks[11], (DEPTH, CONV_DIM, D_MODEL), CONV_DIM),
        'w_o': w(ks[12], (DEPTH, D_MODEL, D_MODEL), D_MODEL),
        'norm_ffn': gain(ks[13], (DEPTH, D_MODEL)),
        'w_ffn_in': w(ks[14], (DEPTH, D_MODEL, 2 * D_FF), D_MODEL),
        'w_ffn_out': w(ks[15], (DEPTH, D_FF, D_MODEL), D_FF),
        'norm_final': gain(ks[16], (D_MODEL,)),
    }


def reference(x_prompt, x_sample, cache_k, cache_v, cache_ki, state_conv, page_table,
              norm_mix, w_in, w_conv, w_attn_out, w_conv_out, w_o, norm_ffn, w_ffn_in, w_ffn_out, norm_final):
    pos_p = jnp.arange(x_prompt.shape[1])
    prefix_p = jnp.zeros((DEPTH, x_prompt.shape[0], CONV_WIDTH - 1, CONV_DIM), x_prompt.dtype)
    y_prompt, new_k_p, new_v_p, new_ki_p, new_conv_p = run_trunk(
        x_prompt, pos_p, lambda l: prompt_sparse_attention, prefix_p, norm_mix, w_in, w_conv,
        w_attn_out, w_conv_out, w_o, norm_ffn, w_ffn_in, w_ffn_out, norm_final)

    pos_s = PAST_LEN + jnp.arange(x_sample.shape[1])

    def sample_attn_for(l):
        return functools.partial(sample_sparse_attention, cache_k=cache_k[l], cache_v=cache_v[l],
                                 cache_ki=cache_ki[l], page_table=page_table)

    y_sample, new_k_s, new_v_s, new_ki_s, new_conv_s = run_trunk(
        x_sample, pos_s, sample_attn_for, state_conv, norm_mix, w_in, w_conv,
        w_attn_out, w_conv_out, w_o, norm_ffn, w_ffn_in, w_ffn_out, norm_final)

    return (y_prompt, y_sample, new_k_p, new_v_p, new_ki_p, new_conv_p,
            new_k_s, new_v_s, new_ki_s, new_conv_s)
```

```python
import functools

import jax
import jax.numpy as jnp
from jax import lax
from jax.experimental import pallas as pl
from jax.experimental.pallas import tpu as pltpu

N_HEADS = 16
N_KV_HEADS = 4
HEAD_DIM = 128
GROUP = N_HEADS // N_KV_HEADS
IDX_HEADS = 16
IDX_DIM = 64
TOPK_MAX = 256
CONV_WIDTH = 3
ROPE_THETA = 10000.0
NORM_EPS = 1e-6
IDX_SCALE = IDX_HEADS ** -0.5 * IDX_DIM ** -0.5
ATTN_SCALE = HEAD_DIM ** -0.5

LANES = 128
SUBLANES = 8
VMEM_LIMIT = 48 * 1024 * 1024
MASKED = -1e30
MAX_BISECT = 200

F32 = jnp.float32
BF16 = jnp.bfloat16


def _cparams(n_grid):
    return pltpu.CompilerParams(
        dimension_semantics=("arbitrary",) * n_grid, vmem_limit_bytes=VMEM_LIMIT)


def _row_tile(t):
    return min(512, t)


def _rms_body(x, g):
    y = x * lax.rsqrt(jnp.mean(x * x, axis=-1, keepdims=True) + NORM_EPS)
    return y * g


def _rmsnorm_kernel(x_ref, g_ref, o_ref):
    o_ref[...] = _rms_body(x_ref[...], g_ref[...]).astype(o_ref.dtype)


def _rmsnorm(x, g, out_dtype):
    t, d = x.shape
    tm = _row_tile(t)
    return pl.pallas_call(
        _rmsnorm_kernel,
        grid=(t // tm,),
        in_specs=[pl.BlockSpec((tm, d), lambda i: (i, 0)),
                  pl.BlockSpec((1, d), lambda i: (0, 0))],
        out_specs=pl.BlockSpec((tm, d), lambda i: (i, 0)),
        out_shape=jax.ShapeDtypeStruct((t, d), out_dtype),
        compiler_params=_cparams(1),
        name="rmsnorm",
    )(x, g.reshape(1, d))


def _rope128(z, cos, sin):
    outs = []
    for c in range(z.shape[1] // LANES):
        zc = z[:, c * LANES:(c + 1) * LANES]
        outs.append(zc * cos + pltpu.roll(zc, HEAD_DIM // 2, 1) * sin)
    return outs[0] if len(outs) == 1 else jnp.concatenate(outs, axis=1)


def _rope64(z, cos, sin):
    lane = lax.broadcasted_iota(jnp.int32, (1, LANES), 1)
    first_half = (lane % IDX_DIM) < (IDX_DIM // 2)
    outs = []
    for c in range(z.shape[1] // LANES):
        zc = z[:, c * LANES:(c + 1) * LANES]
        rot = jnp.where(first_half,
                        pltpu.roll(zc, LANES - IDX_DIM // 2, 1),
                        pltpu.roll(zc, IDX_DIM // 2, 1))
        outs.append(zc * cos + rot * sin)
    return outs[0] if len(outs) == 1 else jnp.concatenate(outs, axis=1)


def _proj_rope_kernel(x_ref, w_ref, cos_ref, sin_ref, *o_refs, rope, scale):
    z = jnp.dot(x_ref[...], w_ref[...], preferred_element_type=F32)
    z = rope(z, cos_ref[...], sin_ref[...])
    if scale != 1.0:
        z = z * scale
    for o in o_refs:
        o[...] = z.astype(o.dtype)


def _proj_plain_kernel(x_ref, w_ref, *o_refs):
    z = jnp.dot(x_ref[...], w_ref[...], preferred_element_type=F32)
    for o in o_refs:
        o[...] = z.astype(o.dtype)


def _proj_gate_kernel(x_ref, w_ref, o_ref):
    z = jnp.dot(x_ref[...], w_ref[...], preferred_element_type=F32)
    o_ref[...] = jax.nn.sigmoid(z).astype(o_ref.dtype)


def _proj_conv_kernel(x_ref, wb_ref, wc_ref, wh_ref, cb_ref, u_ref):
    x = x_ref[...]
    cb_ref[...] = jnp.dot(x, wb_ref[...], preferred_element_type=F32)
    cc = jnp.dot(x, wc_ref[...], preferred_element_type=F32)
    ch = jnp.dot(x, wh_ref[...], preferred_element_type=F32)
    u_ref[...] = cc * ch


def _proj(kernel, xn, ws, tables, out_dtypes, tn, name):
    t, k = xn.shape
    n = ws[0].shape[1]
    tm = _row_tile(t)
    tn = min(tn, n)
    in_specs = [pl.BlockSpec((tm, k), lambda i, j: (i, 0))]
    in_specs += [pl.BlockSpec((k, tn), lambda i, j: (0, j)) for _ in ws]
    in_specs += [pl.BlockSpec((tm, LANES), lambda i, j: (i, 0)) for _ in tables]
    outs = pl.pallas_call(
        kernel,
        grid=(t // tm, n // tn),
        in_specs=in_specs,
        out_specs=[pl.BlockSpec((tm, tn), lambda i, j: (i, j)) for _ in out_dtypes],
        out_shape=[jax.ShapeDtypeStruct((t, n), dt) for dt in out_dtypes],
        compiler_params=_cparams(2),
        name=name,
    )(xn, *ws, *tables)
    return outs


def _bisect_threshold(count_ge, lo0, hi0, need, kf):
    big = jnp.maximum(jnp.abs(hi0) * 1e-6, 1e-30)
    hi0 = hi0 + big
    done0 = jnp.where(need, 0.0, 1.0)
    zeros = jnp.zeros_like(lo0)

    def cond(c):
        it, _, _, _, done, _ = c
        return jnp.logical_and(it < MAX_BISECT, jnp.min(done) < 0.5)

    def body(c):
        it, lo, hi, thr, done, tie = c
        mid = 0.5 * lo + 0.5 * hi
        cnt = count_ge(mid)
        live = done < 0.5
        collapsed = jnp.logical_or(mid <= lo, mid >= hi)
        found = jnp.logical_and(cnt == kf, jnp.logical_not(collapsed))
        fin = jnp.logical_and(live, jnp.logical_or(found, collapsed))
        thr = jnp.where(jnp.logical_and(live, found), mid, thr)
        thr = jnp.where(jnp.logical_and(live, collapsed), lo, thr)
        tie = jnp.where(jnp.logical_and(live, collapsed), 1.0, tie)
        done = jnp.where(fin, 1.0, done)
        go_up = jnp.logical_and(live, cnt > kf)
        go_dn = jnp.logical_and(live, cnt < kf)
        lo = jnp.where(go_up, mid, lo)
        hi = jnp.where(go_dn, mid, hi)
        return it + 1, lo, hi, thr, done, tie

    _, lo, _, thr, done, tie = lax.while_loop(
        cond, body, (jnp.int32(0), lo0, hi0, lo0, done0, zeros))
    thr = jnp.where(done > 0.5, thr, lo)
    return thr, tie


def _prompt_attn_kernel(qT_ref, qiT_ref, wT_ref, ki_ref, k_ref, vT_ref, o_ref,
                        sc_ref, acc_ref, m_ref, l_ref, *, tq, topk, seq):
    i = pl.program_id(0)
    nck = i + 1
    qpos = i * tq + lax.broadcasted_iota(jnp.int32, (1, tq), 1)
    row = lax.broadcasted_iota(jnp.int32, (tq, 1), 0)
    kf = float(topk)

    def score_chunk(c, carry):
        mx, mn = carry
        off = pl.multiple_of(c * tq, tq)
        kic = ki_ref[pl.ds(off, tq), :]
        acc = jnp.zeros((tq, tq), F32)
        for h in range(IDX_HEADS):
            d = jnp.dot(kic, qiT_ref[h * IDX_DIM:(h + 1) * IDX_DIM, :],
                        preferred_element_type=F32)
            acc = acc + jnp.maximum(d, 0.0) * (wT_ref[h:h + 1, :] * IDX_SCALE)
        valid = (off + row) <= qpos
        sc_ref[pl.ds(off, tq), :] = jnp.where(valid, acc, -jnp.inf)
        mx = jnp.maximum(mx, jnp.max(jnp.where(valid, acc, -jnp.inf), axis=0, keepdims=True))
        mn = jnp.minimum(mn, jnp.min(jnp.where(valid, acc, jnp.inf), axis=0, keepdims=True))
        return mx, mn

    mx, mn = lax.fori_loop(
        0, nck, score_chunk,
        (jnp.full((1, tq), -jnp.inf, F32), jnp.full((1, tq), jnp.inf, F32)))

    def count_where(pred):
        def body(c, acc):
            off = pl.multiple_of(c * tq, tq)
            ind = jnp.where(pred(sc_ref[pl.ds(off, tq), :], off + row), 1.0, 0.0)
            return acc + jnp.sum(ind.reshape(tq // SUBLANES, SUBLANES, tq), axis=0)
        acc = lax.fori_loop(0, nck, body, jnp.zeros((SUBLANES, tq), F32))
        return jnp.sum(acc, axis=0, keepdims=True)

    need = (qpos + 1) > topk
    thr, tie = _bisect_threshold(
        lambda t: count_where(lambda s, kp: s >= t), mn, mx, need, kf)
    thr = jnp.where(need, thr, -jnp.inf)

    any_tie = jnp.max(tie) > 0.5
    nbits = max(1, (seq - 1).bit_length())

    def tie_search(_):
        c_gt = count_where(lambda s, kp: s > thr)
        r = kf - c_gt

        def body(b, j):
            cand = j + jnp.left_shift(jnp.int32(1), nbits - 1 - b)
            cnt = count_where(lambda s, kp: jnp.logical_and(s == thr, kp < cand))
            return jnp.where(cnt < r, cand, j)
        j = lax.fori_loop(0, nbits, body, jnp.zeros((1, tq), jnp.int32))
        return jnp.where(tie > 0.5, j, jnp.int32(seq))

    jlim = lax.cond(any_tie, tie_search,
                    lambda _: jnp.full((1, tq), seq, jnp.int32), 0)
    jlim = jnp.where(need, jlim, -1)

    m_ref[...] = jnp.full(m_ref.shape, MASKED, F32)
    l_ref[...] = jnp.zeros(l_ref.shape, F32)
    acc_ref[...] = jnp.zeros(acc_ref.shape, F32)

    def attn_chunk(c, carry):
        off = pl.multiple_of(c * tq, tq)
        s = sc_ref[pl.ds(off, tq), :]
        kpos = off + row
        sel = jnp.logical_or(s > thr, jnp.logical_and(s == thr, kpos <= jlim))
        bias = jnp.where(sel, 0.0, MASKED)
        bias4 = jnp.concatenate([bias] * GROUP, axis=1)
        for g in range(N_KV_HEADS):
            kc = k_ref[pl.ds(off, tq), g * HEAD_DIM:(g + 1) * HEAD_DIM]
            qg = jnp.concatenate(
                [qT_ref[(g * GROUP + j) * HEAD_DIM:(g * GROUP + j + 1) * HEAD_DIM, :]
                 for j in range(GROUP)], axis=1)
            st = jnp.dot(kc, qg, preferred_element_type=F32) + bias4
            m_old = m_ref[g]
            m_new = jnp.maximum(m_old, jnp.max(st, axis=0, keepdims=True))
            alpha = jnp.exp(m_old - m_new)
            p = jnp.exp(st - m_new)
            l_ref[g] = alpha * l_ref[g] + jnp.sum(p, axis=0, keepdims=True)
            pv = jnp.dot(vT_ref[g * HEAD_DIM:(g + 1) * HEAD_DIM, pl.ds(off, tq)],
                         p.astype(BF16), preferred_element_type=F32)
            acc_ref[g] = acc_ref[g] * alpha + pv
            m_ref[g] = m_new
        return carry

    lax.fori_loop(0, nck, attn_chunk, 0)

    for g in range(N_KV_HEADS):
        inv = 1.0 / l_ref[g]
        og = acc_ref[g] * inv
        for j in range(GROUP):
            h = g * GROUP + j
            o_ref[:, h * HEAD_DIM:(h + 1) * HEAD_DIM] = (
                og[:, j * tq:(j + 1) * tq].T.astype(o_ref.dtype))


def _prompt_attention(qT, qiT, wT, ki, kb, vT, tq):
    seq = kb.shape[0]
    topk = min(TOPK_MAX, seq // 4)
    const = lambda shape: pl.BlockSpec(shape, lambda i: (0, 0), pipeline_mode=pl.Buffered(1))
    return pl.pallas_call(
        functools.partial(_prompt_attn_kernel, tq=tq, topk=topk, seq=seq),
        grid=(seq // tq,),
        in_specs=[pl.BlockSpec((N_HEADS * HEAD_DIM, tq), lambda i: (0, i)),
                  pl.BlockSpec((IDX_HEADS * IDX_DIM, tq), lambda i: (0, i)),
                  pl.BlockSpec((IDX_HEADS, tq), lambda i: (0, i)),
                  const((seq, IDX_DIM)),
                  const((seq, N_KV_HEADS * HEAD_DIM)),
                  const((N_KV_HEADS * HEAD_DIM, seq))],
        out_specs=pl.BlockSpec((tq, N_HEADS * HEAD_DIM), lambda i: (i, 0)),
        out_shape=jax.ShapeDtypeStruct((seq, N_HEADS * HEAD_DIM), BF16),
        scratch_shapes=[pltpu.VMEM((seq, tq), F32),
                        pltpu.VMEM((N_KV_HEADS, HEAD_DIM, GROUP * tq), F32),
                        pltpu.VMEM((N_KV_HEADS, 1, GROUP * tq), F32),
                        pltpu.VMEM((N_KV_HEADS, 1, GROUP * tq), F32)],
        compiler_params=_cparams(1),
        name="prompt_attention",
    )(qT, qiT, wT, ki, kb, vT)


def _sample_attn_kernel(pt_ref, qi_ref, w_ref, q_ref, kin_ref, kn_ref, vn_ref, *rest,
                        n_pages, page, t_new, topk):
    ki_refs = rest[:n_pages]
    k_refs = rest[n_pages:2 * n_pages]
    v_refs = rest[2 * n_pages:3 * n_pages]
    o_ref = rest[3 * n_pages]
    sc_ref, s_ref = rest[3 * n_pages + 1:]
    del pt_ref
    past = n_pages * page
    nchunk = n_pages + 1
    total = nchunk * page
    kf = float(topk)
    nt = (((1,), (1,)), ((), ()))

    def pad_rows(x):
        return jnp.concatenate(
            [x, jnp.zeros((page - x.shape[0], x.shape[1]), x.dtype)], axis=0).astype(BF16)

    lane = lax.broadcasted_iota(jnp.int32, (1, page), 1)
    trow = lax.broadcasted_iota(jnp.int32, (t_new, 1), 0)
    new_valid = lane <= trow

    qi = qi_ref[...]
    wb = jnp.broadcast_to(w_ref[...] * IDX_SCALE, (IDX_HEADS * t_new, page))

    def idx_scores(kip):
        d = lax.dot_general(qi, kip, nt, preferred_element_type=F32)
        r = jnp.maximum(d, 0.0) * wb
        return jnp.sum(r.reshape(IDX_HEADS, t_new, page), axis=0)

    for p in range(n_pages):
        sc_ref[:, p * page:(p + 1) * page] = idx_scores(ki_refs[p][...].astype(BF16))
    s_new = idx_scores(pad_rows(kin_ref[...]))
    sc_ref[:, past:total] = jnp.where(new_valid, s_new, -jnp.inf)

    def count_where(pred):
        acc = jnp.zeros((t_new, page), F32)
        for c in range(nchunk):
            acc = acc + jnp.where(pred(sc_ref[:, c * page:(c + 1) * page], c * page + lane),
                                  1.0, 0.0)
        return jnp.sum(acc, axis=1, keepdims=True)

    sc_all = sc_ref[...]
    mx = jnp.max(sc_all, axis=1, keepdims=True)
    mn = jnp.min(jnp.where(sc_all == -jnp.inf, jnp.inf, sc_all), axis=1, keepdims=True)
    need = (past + trow + 1) > topk
    thr, tie = _bisect_threshold(
        lambda t: count_where(lambda s, kp: s >= t), mn, mx, need, kf)
    thr = jnp.where(need, thr, -jnp.inf)

    any_tie = jnp.max(tie) > 0.5
    nbits = max(1, (total - 1).bit_length())

    def tie_search(_):
        c_gt = count_where(lambda s, kp: s > thr)
        r = kf - c_gt

        def body(b, j):
            cand = j + jnp.left_shift(jnp.int32(1), nbits - 1 - b)
            cnt = count_where(lambda s, kp: jnp.logical_and(s == thr, kp < cand))
            return jnp.where(cnt < r, cand, j)
        j = lax.fori_loop(0, nbits, body, jnp.zeros((t_new, 1), jnp.int32))
        return jnp.where(tie > 0.5, j, jnp.int32(total))

    jlim = lax.cond(any_tie, tie_search,
                    lambda _: jnp.full((t_new, 1), total, jnp.int32), 0)
    jlim = jnp.where(need, jlim, -1)

    q = q_ref[...]
    rows_g = GROUP * t_new

    def logits_chunk(c, kc):
        s = sc_ref[:, c * page:(c + 1) * page]
        kpos = c * page + lane
        sel = jnp.logical_or(s > thr, jnp.logical_and(s == thr, kpos <= jlim))
        bias = jnp.where(sel, 0.0, MASKED)
        bias4 = jnp.concatenate([bias] * GROUP, axis=0)
        for g in range(N_KV_HEADS):
            st = lax.dot_general(q[g * rows_g:(g + 1) * rows_g, :],
                                 kc[:, g * HEAD_DIM:(g + 1) * HEAD_DIM], nt,
                                 preferred_element_type=F32)
            s_ref[g * rows_g:(g + 1) * rows_g, c * page:(c + 1) * page] = st + bias4

    for p in range(n_pages):
        logits_chunk(p, k_refs[p][...].astype(BF16))
    logits_chunk(n_pages, pad_rows(kn_ref[...]))

    s_all = s_ref[...]
    m = jnp.max(s_all, axis=1, keepdims=True)
    pr = jnp.exp(s_all - m)
    l = jnp.sum(pr, axis=1, keepdims=True)
    s_ref[...] = pr

    accs = [jnp.zeros((rows_g, HEAD_DIM), F32) for _ in range(N_KV_HEADS)]

    def add_values(c, vc):
        for g in range(N_KV_HEADS):
            pc = s_ref[g * rows_g:(g + 1) * rows_g, c * page:(c + 1) * page].astype(BF16)
            accs[g] = accs[g] + jnp.dot(pc, vc[:, g * HEAD_DIM:(g + 1) * HEAD_DIM],
                                        preferred_element_type=F32)

    for p in range(n_pages):
        add_values(p, v_refs[p][...].astype(BF16))
    add_values(n_pages, pad_rows(vn_ref[...]))

    inv = 1.0 / l
    for g in range(N_KV_HEADS):
        og = accs[g] * inv[g * rows_g:(g + 1) * rows_g, :]
        for j in range(GROUP):
            h = g * GROUP + j
            o_ref[:, h * HEAD_DIM:(h + 1) * HEAD_DIM] = (
                og[j * t_new:(j + 1) * t_new, :].astype(o_ref.dtype))


def _sample_attention(layer, page_table, qi_s, w_s, q_s, ki_new, k_new, v_new,
                      cache_ki, cache_k, cache_v):
    db, n_pages = page_table.shape
    page = cache_k.shape[2]
    t_new = k_new.shape[1]
    total = n_pages * page + t_new
    topk = min(TOPK_MAX, total // 4)
    kvd = N_KV_HEADS * HEAD_DIM
    rows = N_HEADS * t_new

    def per_seq(shape):
        return pl.BlockSpec((None,) + shape, lambda b, pt: (b, 0, 0))

    def paged(width, p):
        return pl.BlockSpec((None, None, page, width),
                            lambda b, pt, p=p: (layer, pt[b, p], 0, 0))

    in_specs = [per_seq((rows, IDX_DIM)), per_seq((rows, 1)), per_seq((rows, HEAD_DIM)),
                per_seq((t_new, IDX_DIM)), per_seq((t_new, kvd)), per_seq((t_new, kvd))]
    in_specs += [paged(IDX_DIM, p) for p in range(n_pages)]
    in_specs += [paged(kvd, p) for p in range(n_pages)]
    in_specs += [paged(kvd, p) for p in range(n_pages)]
    padded = (n_pages + 1) * page
    grid_spec = pltpu.PrefetchScalarGridSpec(
        num_scalar_prefetch=1,
        grid=(db,),
        in_specs=in_specs,
        out_specs=pl.BlockSpec((None, t_new, N_HEADS * HEAD_DIM), lambda b, pt: (b, 0, 0)),
        scratch_shapes=[pltpu.VMEM((t_new, padded), F32),
                        pltpu.VMEM((rows, padded), F32)],
    )
    ck = cache_k.reshape(cache_k.shape[0], cache_k.shape[1], page, kvd)
    cv = cache_v.reshape(cache_v.shape[0], cache_v.shape[1], page, kvd)
    return pl.pallas_call(
        functools.partial(_sample_attn_kernel, n_pages=n_pages, page=page,
                          t_new=t_new, topk=topk),
        grid_spec=grid_spec,
        out_shape=jax.ShapeDtypeStruct((db, t_new, N_HEADS * HEAD_DIM), F32),
        compiler_params=_cparams(1),
        name="sample_attention",
    )(page_table, qi_s, w_s, q_s, ki_new, k_new, v_new,
      *([cache_ki] * n_pages), *([ck] * n_pages), *([cv] * n_pages))


def _conv_kernel(u_ref, halo_ref, cb_ref, w_ref, o_ref, *, seq_len, tm):
    i = pl.program_id(0)
    u = u_ref[...]
    halo = halo_ref[...]
    r = lax.broadcasted_iota(jnp.int32, (tm, 1), 0)
    if seq_len >= tm:
        t = r
        halo = jnp.where(i > 0, halo, 0.0)
        h7 = halo[SUBLANES - 1:SUBLANES, :]
        h6 = halo[SUBLANES - 2:SUBLANES - 1, :]
        pre1 = jnp.broadcast_to(h7, u.shape)
        pre2 = jnp.where(r == 0, h6, h7)
    else:
        t = r % seq_len
        pre2 = halo
        pre1 = pltpu.roll(halo, tm - 1, 0)
    u1 = jnp.where(t >= 1, pltpu.roll(u, 1, 0), pre1)
    u2 = jnp.where(t >= 2, pltpu.roll(u, 2, 0), pre2)
    y = w_ref[0:1, :] * u2 + w_ref[1:2, :] * u1 + w_ref[2:3, :] * u
    o_ref[...] = (cb_ref[...] * y).astype(o_ref.dtype)


def _conv_branch(u, cb, w_conv, history, seq_len):
    t, c = u.shape
    tm = _row_tile(t)
    if history is None:
        halo_arr = u
        halo_spec = pl.BlockSpec(
            (SUBLANES, c), lambda i: (jnp.maximum(i * (tm // SUBLANES) - 1, 0), 0))
    else:
        halo_arr = history
        halo_spec = pl.BlockSpec((tm, c), lambda i: (i, 0))
    return pl.pallas_call(
        functools.partial(_conv_kernel, seq_len=seq_len, tm=tm),
        grid=(t // tm,),
        in_specs=[pl.BlockSpec((tm, c), lambda i: (i, 0)), halo_spec,
                  pl.BlockSpec((tm, c), lambda i: (i, 0)),
                  pl.BlockSpec((CONV_WIDTH, c), lambda i: (0, 0))],
        out_specs=pl.BlockSpec((tm, c), lambda i: (i, 0)),
        out_shape=jax.ShapeDtypeStruct((t, c), BF16),
        compiler_params=_cparams(1),
        name="short_conv",
    )(u, halo_arr, cb, w_conv)


def _merge_kernel(a_ref, c_ref, wa_ref, wc_ref, ga_ref, gb_ref, o_ref):
    a = jnp.dot(a_ref[...], wa_ref[...], preferred_element_type=F32)
    c = jnp.dot(c_ref[...], wc_ref[...], preferred_element_type=F32)
    o_ref[...] = (ga_ref[...] * a + gb_ref[...] * c).astype(o_ref.dtype)


def _merge(a, cpre, w_ao, w_co, gates, tn=512):
    t, d = a.shape[0], w_ao.shape[1]
    tm = _row_tile(t)
    nj = d // tn
    return pl.pallas_call(
        _merge_kernel,
        grid=(t // tm, nj),
        in_specs=[pl.BlockSpec((tm, a.shape[1]), lambda i, j: (i, 0)),
                  pl.BlockSpec((tm, cpre.shape[1]), lambda i, j: (i, 0)),
                  pl.BlockSpec((w_ao.shape[0], tn), lambda i, j: (0, j)),
                  pl.BlockSpec((w_co.shape[0], tn), lambda i, j: (0, j)),
                  pl.BlockSpec((tm, tn), lambda i, j: (i, j)),
                  pl.BlockSpec((tm, tn), lambda i, j: (i, j + nj))],
        out_specs=pl.BlockSpec((tm, tn), lambda i, j: (i, j)),
        out_shape=jax.ShapeDtypeStruct((t, d), BF16),
        compiler_params=_cparams(2),
        name="gated_merge",
    )(a, cpre, w_ao, w_co, gates, gates)


def _out_proj_kernel(m_ref, w_ref, x_ref, g_ref, x1_ref, xn_ref):
    h = jnp.dot(m_ref[...], w_ref[...], preferred_element_type=F32)
    x1 = x_ref[...] + h
    x1_ref[...] = x1
    xn_ref[...] = _rms_body(x1, g_ref[...]).astype(xn_ref.dtype)


def _out_proj(merged, w_o, x, g):
    t, d = x.shape
    tm = _row_tile(t)
    return pl.pallas_call(
        _out_proj_kernel,
        grid=(t // tm,),
        in_specs=[pl.BlockSpec((tm, d), lambda i: (i, 0)),
                  pl.BlockSpec((d, d), lambda i: (0, 0), pipeline_mode=pl.Buffered(1)),
                  pl.BlockSpec((tm, d), lambda i: (i, 0)),
                  pl.BlockSpec((1, d), lambda i: (0, 0))],
        out_specs=[pl.BlockSpec((tm, d), lambda i: (i, 0)),
                   pl.BlockSpec((tm, d), lambda i: (i, 0))],
        out_shape=[jax.ShapeDtypeStruct((t, d), F32), jax.ShapeDtypeStruct((t, d), BF16)],
        compiler_params=_cparams(1),
        name="out_proj",
    )(merged, w_o, x, g.reshape(1, d))


def _ffn_in_kernel(x_ref, wg_ref, wu_ref, o_ref):
    x = x_ref[...]
    g = jnp.dot(x, wg_ref[...], preferred_element_type=F32)
    u = jnp.dot(x, wu_ref[...], preferred_element_type=F32)
    o_ref[...] = (jax.nn.silu(g) * u).astype(o_ref.dtype)


def _ffn_in(xn, w_fi, tn=512):
    t, d = xn.shape
    d_ff = w_fi.shape[1] // 2
    tm = _row_tile(t)
    nj = d_ff // tn
    return pl.pallas_call(
        _ffn_in_kernel,
        grid=(t // tm, nj),
        in_specs=[pl.BlockSpec((tm, d), lambda i, j: (i, 0)),
                  pl.BlockSpec((d, tn), lambda i, j: (0, j)),
                  pl.BlockSpec((d, tn), lambda i, j: (0, j + nj))],
        out_specs=pl.BlockSpec((tm, tn), lambda i, j: (i, j)),
        out_shape=jax.ShapeDtypeStruct((t, d_ff), BF16),
        compiler_params=_cparams(2),
        name="ffn_in",
    )(xn, w_fi, w_fi)


def _ffn_out_kernel(a_ref, w_ref, x_ref, o_ref):
    o_ref[...] = x_ref[...] + jnp.dot(a_ref[...], w_ref[...], preferred_element_type=F32)


def _ffn_out(act, w_fo, x1, tn=512):
    t, d = x1.shape
    d_ff = act.shape[1]
    tm = _row_tile(t)
    return pl.pallas_call(
        _ffn_out_kernel,
        grid=(t // tm, d // tn),
        in_specs=[pl.BlockSpec((tm, d_ff), lambda i, j: (i, 0)),
                  pl.BlockSpec((d_ff, tn), lambda i, j: (0, j)),
                  pl.BlockSpec((tm, tn), lambda i, j: (i, j))],
        out_specs=pl.BlockSpec((tm, tn), lambda i, j: (i, j)),
        out_shape=jax.ShapeDtypeStruct((t, d), F32),
        compiler_params=_cparams(2),
        name="ffn_out",
    )(act, w_fo, x1)


def _rope_tables(pos, dim, reps):
    inv = ROPE_THETA ** (-jnp.arange(0, dim, 2, dtype=F32) / dim)
    ang = pos.astype(F32)[:, None] * inv[None, :]
    cos, sin = jnp.cos(ang), jnp.sin(ang)
    return (jnp.tile(jnp.concatenate([cos, cos], axis=1), (1, reps)),
            jnp.tile(jnp.concatenate([-sin, sin], axis=1), (1, reps)))


def _split_w_in(w, d_model):
    attn, kv, conv = N_HEADS * HEAD_DIM, N_KV_HEADS * HEAD_DIM, d_model // 2
    sizes = (attn, kv, kv, IDX_HEADS * IDX_DIM, IDX_DIM, IDX_HEADS, conv, conv, conv,
             d_model, d_model)
    parts, off = [], 0
    for n in sizes:
        parts.append(w[:, off:off + n].astype(BF16))
        off += n
    wq, wk, wv, wqi, wki, wwi, wcb, wcc, wch, wga, wgb = parts
    pad = jnp.zeros((w.shape[0], LANES - IDX_DIM - IDX_HEADS), BF16)
    return dict(q=wq, k=wk, v=wv, qi=wqi, kiwi=jnp.concatenate([wki, wwi, pad], axis=1),
                cb=wcb, cc=wcc, ch=wch, gates=jnp.concatenate([wga, wgb], axis=1))


def _layer(x, wl, tabs, attn_fn, history, seq_len):
    cos128, sin128, cos64, sin64, cos_kw, sin_kw = tabs
    xn = _rmsnorm(x, wl["norm_mix"], BF16)
    rope128 = functools.partial(_proj_rope_kernel, rope=_rope128)
    rope64 = functools.partial(_proj_rope_kernel, rope=_rope64)
    (q,) = _proj(functools.partial(rope128, scale=ATTN_SCALE), xn, [wl["q"]],
                 [cos128, sin128], [BF16], 512, "proj_q")
    k, kb = _proj(functools.partial(rope128, scale=1.0), xn, [wl["k"]],
                  [cos128, sin128], [F32, BF16], 512, "proj_k")
    v, vb = _proj(_proj_plain_kernel, xn, [wl["v"]], [], [F32, BF16], 512, "proj_v")
    (qi,) = _proj(functools.partial(rope64, scale=1.0), xn, [wl["qi"]],
                  [cos64, sin64], [BF16], 512, "proj_qi")
    (kiwi,) = _proj(functools.partial(rope64, scale=1.0), xn, [wl["kiwi"]],
                    [cos_kw, sin_kw], [F32], LANES, "proj_kiwi")
    cb, u = _proj(_proj_conv_kernel, xn, [wl["cb"], wl["cc"], wl["ch"]], [],
                  [F32, F32], 512, "proj_conv")
    (gates,) = _proj(_proj_gate_kernel, xn, [wl["gates"]], [], [F32], 512, "proj_gates")
    ki = kiwi[:, :IDX_DIM]
    wi = kiwi[:, IDX_DIM:IDX_DIM + IDX_HEADS]

    a = attn_fn(q=q, k=k, kb=kb, v=v, vb=vb, qi=qi, ki=ki, wi=wi)
    cpre = _conv_branch(u, cb, wl["w_conv"], history, seq_len)
    merged = _merge(a, cpre, wl["w_attn_out"], wl["w_conv_out"], gates)
    x1, xn2 = _out_proj(merged, wl["w_o"], x, wl["norm_ffn"])
    act = _ffn_in(xn2, wl["w_ffn_in"])
    x2 = _ffn_out(act, wl["w_ffn_out"], x1)
    return x2, k, v, ki, u


def kernel(x_prompt, x_sample, cache_k, cache_v, cache_ki, state_conv, page_table,
           norm_mix, w_in, w_conv, w_attn_out, w_conv_out, w_o, norm_ffn, w_ffn_in,
           w_ffn_out, norm_final):
    depth = w_in.shape[0]
    bp, seq, d_model = x_prompt.shape
    db, t_new, _ = x_sample.shape
    page = cache_k.shape[2]
    past = page_table.shape[1] * page
    conv_dim = d_model // 2
    kvd = N_KV_HEADS * HEAD_DIM

    pos_p = jnp.arange(seq)
    pos_s = jnp.tile(past + jnp.arange(t_new), db)

    def tables(pos):
        c128, s128 = _rope_tables(pos, HEAD_DIM, 1)
        c64, s64 = _rope_tables(pos, IDX_DIM, LANES // IDX_DIM)
        keep = jnp.arange(LANES)[None, :] < IDX_DIM
        return (c128, s128, c64, s64, jnp.where(keep, c64, 1.0), jnp.where(keep, s64, 0.0))

    tabs_p, tabs_s = tables(pos_p), tables(pos_s)

    layers = []
    for l in range(depth):
        wl = _split_w_in(w_in[l], d_model)
        wl.update(norm_mix=norm_mix[l], norm_ffn=norm_ffn[l], w_conv=w_conv[l],
                  w_attn_out=w_attn_out[l].astype(BF16),
                  w_conv_out=w_conv_out[l].astype(BF16), w_o=w_o[l].astype(BF16),
                  w_ffn_in=w_ffn_in[l].astype(BF16), w_ffn_out=w_ffn_out[l].astype(BF16))
        layers.append(wl)

    tq = min(256, seq)

    def prompt_attn(q, k, kb, v, vb, qi, ki, wi):
        del k, v
        return _prompt_attention(q.T, qi.T, wi.T, ki.astype(BF16), kb, vb.T, tq)

    def head_major(a, width):
        nh = a.shape[1] // width
        return a.reshape(db, t_new, nh, width).transpose(0, 2, 1, 3).reshape(
            db, nh * t_new, width)

    def sample_attn_for(l):
        def fn(q, k, kb, v, vb, qi, ki, wi):
            del kb, vb
            a = _sample_attention(
                l, page_table, head_major(qi, IDX_DIM), head_major(wi, 1),
                head_major(q, HEAD_DIM), ki.reshape(db, t_new, IDX_DIM),
                k.reshape(db, t_new, kvd), v.reshape(db, t_new, kvd),
                cache_ki, cache_k, cache_v)
            return a.reshape(db * t_new, N_HEADS * HEAD_DIM).astype(BF16)
        return fn

    def run(x, tabs, make_attn, history_for, seq_len):
        ks, vs, kis, us = [], [], [], []
        for l in range(depth):
            x, k, v, ki, u = _layer(x, layers[l], tabs, make_attn(l),
                                            history_for(l), seq_len)
            ks.append(k)
            vs.append(v)
            kis.append(ki)
            us.append(u)
        return _rmsnorm(x, norm_final, F32), ks, vs, kis, us

    assert bp == 1
    y_p, ks, vs, kis, us = run(x_prompt[0], tabs_p, lambda l: prompt_attn,
                               lambda l: None, seq)
    y_prompt = y_p[None]
    new_k_p = jnp.stack(ks).reshape(depth, bp, seq, N_KV_HEADS, HEAD_DIM)
    new_v_p = jnp.stack(vs).reshape(depth, bp, seq, N_KV_HEADS, HEAD_DIM)
    new_ki_p = jnp.stack(kis).reshape(depth, bp, seq, IDX_DIM)
    new_conv_p = jnp.stack([u[seq - (CONV_WIDTH - 1):] for u in us]).reshape(
        depth, bp, CONV_WIDTH - 1, conv_dim)

    def history_for(l):
        h = state_conv[l]
        pad = jnp.zeros((db, t_new - (CONV_WIDTH - 1), conv_dim), F32)
        return jnp.concatenate([h, pad], axis=1).reshape(db * t_new, conv_dim)

    y_s, ks, vs, kis, us = run(x_sample.reshape(db * t_new, d_model), tabs_s,
                               sample_attn_for, history_for, t_new)
    y_sample = y_s.reshape(db, t_new, d_model)
    new_k_s = jnp.stack(ks).reshape(depth, db, t_new, N_KV_HEADS, HEAD_DIM)
    new_v_s = jnp.stack(vs).reshape(depth, db, t_new, N_KV_HEADS, HEAD_DIM)
    new_ki_s = jnp.stack(kis).reshape(depth, db, t_new, IDX_DIM)
    new_conv_s = jnp.stack(
        [u.reshape(db, t_new, conv_dim)[:, t_new - (CONV_WIDTH - 1):] for u in us])

    return (y_prompt, y_sample, new_k_p, new_v_p, new_ki_p, new_conv_p,
            new_k_s, new_v_s, new_ki_s, new_conv_s)
```

```python
import functools

import jax
import jax.numpy as jnp
from jax import lax
from jax.experimental import pallas as pl
from jax.experimental.pallas import tpu as pltpu

N_HEADS = 16
N_KV_HEADS = 4
HEAD_DIM = 128
GROUP = N_HEADS // N_KV_HEADS
IDX_HEADS = 16
IDX_DIM = 64
TOPK_MAX = 256
CONV_WIDTH = 3
ROPE_THETA = 10000.0
NORM_EPS = 1e-6
IDX_SCALE = IDX_HEADS ** -0.5 * IDX_DIM ** -0.5
LOG2_E = 1.4426950408889634
Q_SCALE = HEAD_DIM ** -0.5 * LOG2_E

LANES = 128
SUBLANES = 8
VMEM_LIMIT = 48 * 1024 * 1024
MASKED = -1e30
MAX_BISECT = 200
HEADS_PER_UNIT = 2
SUM_FLOOR = 2.0 ** -100

F32 = jnp.float32
BF16 = jnp.bfloat16


def _cparams(n_grid):
    return pltpu.CompilerParams(
        dimension_semantics=("arbitrary",) * n_grid, vmem_limit_bytes=VMEM_LIMIT)


def _row_tile(t):
    return min(512, t)


def _rms_body(x, g):
    y = x * lax.rsqrt(jnp.mean(x * x, axis=-1, keepdims=True) + NORM_EPS)
    return y * g


def _rmsnorm_kernel(x_ref, g_ref, o_ref):
    o_ref[...] = _rms_body(x_ref[...], g_ref[...]).astype(o_ref.dtype)


def _rmsnorm(x, g, out_dtype):
    t, d = x.shape
    tm = _row_tile(t)
    return pl.pallas_call(
        _rmsnorm_kernel,
        grid=(t // tm,),
        in_specs=[pl.BlockSpec((tm, d), lambda i: (i, 0)),
                  pl.BlockSpec((1, d), lambda i: (0, 0))],
        out_specs=pl.BlockSpec((tm, d), lambda i: (i, 0)),
        out_shape=jax.ShapeDtypeStruct((t, d), out_dtype),
        compiler_params=_cparams(1),
        name="rmsnorm",
    )(x, g.reshape(1, d))


def _rope128(z, cos, sin):
    outs = []
    for c in range(z.shape[1] // LANES):
        zc = z[:, c * LANES:(c + 1) * LANES]
        outs.append(zc * cos + pltpu.roll(zc, HEAD_DIM // 2, 1) * sin)
    return outs[0] if len(outs) == 1 else jnp.concatenate(outs, axis=1)


def _rope64(z, cos, sin):
    lane = lax.broadcasted_iota(jnp.int32, (1, LANES), 1)
    first_half = (lane % IDX_DIM) < (IDX_DIM // 2)
    outs = []
    for c in range(z.shape[1] // LANES):
        zc = z[:, c * LANES:(c + 1) * LANES]
        rot = jnp.where(first_half,
                        pltpu.roll(zc, LANES - IDX_DIM // 2, 1),
                        pltpu.roll(zc, IDX_DIM // 2, 1))
        outs.append(zc * cos + rot * sin)
    return outs[0] if len(outs) == 1 else jnp.concatenate(outs, axis=1)


def _proj_rope_kernel(x_ref, w_ref, cos_ref, sin_ref, *o_refs, rope, scale):
    z = jnp.dot(x_ref[...], w_ref[...], preferred_element_type=F32)
    z = rope(z, cos_ref[...], sin_ref[...])
    if scale != 1.0:
        z = z * scale
    for o in o_refs:
        o[...] = z.astype(o.dtype)


def _proj_plain_kernel(x_ref, w_ref, *o_refs):
    z = jnp.dot(x_ref[...], w_ref[...], preferred_element_type=F32)
    for o in o_refs:
        o[...] = z.astype(o.dtype)


def _proj_gate_kernel(x_ref, w_ref, o_ref):
    z = jnp.dot(x_ref[...], w_ref[...], preferred_element_type=F32)
    o_ref[...] = jax.nn.sigmoid(z).astype(o_ref.dtype)


def _proj_conv_kernel(x_ref, wb_ref, wc_ref, wh_ref, cb_ref, u_ref):
    x = x_ref[...]
    cb_ref[...] = jnp.dot(x, wb_ref[...], preferred_element_type=F32)
    cc = jnp.dot(x, wc_ref[...], preferred_element_type=F32)
    ch = jnp.dot(x, wh_ref[...], preferred_element_type=F32)
    u_ref[...] = cc * ch


def _proj(kernel, xn, ws, tables, out_dtypes, tn, name):
    t, k = xn.shape
    n = ws[0].shape[1]
    tm = _row_tile(t)
    tn = min(tn, n)
    in_specs = [pl.BlockSpec((tm, k), lambda i, j: (i, 0))]
    in_specs += [pl.BlockSpec((k, tn), lambda i, j: (0, j)) for _ in ws]
    in_specs += [pl.BlockSpec((tm, LANES), lambda i, j: (i, 0)) for _ in tables]
    outs = pl.pallas_call(
        kernel,
        grid=(t // tm, n // tn),
        in_specs=in_specs,
        out_specs=[pl.BlockSpec((tm, tn), lambda i, j: (i, j)) for _ in out_dtypes],
        out_shape=[jax.ShapeDtypeStruct((t, n), dt) for dt in out_dtypes],
        compiler_params=_cparams(2),
        name=name,
    )(xn, *ws, *tables)
    return outs


def _bisect_threshold(count_ge, lo0, hi0, need, kf):
    big = jnp.maximum(jnp.abs(hi0) * 1e-6, 1e-30)
    hi0 = hi0 + big
    done0 = jnp.where(need, 0.0, 1.0)
    zeros = jnp.zeros_like(lo0)

    def cond(c):
        it, pending = c[0], c[1]
        return jnp.logical_and(it < MAX_BISECT, pending > 0)

    def body(c):
        it, _, lo, hi, thr, done, tie = c
        pending = (jnp.min(done) < 0.5).astype(jnp.int32)
        mid = 0.5 * lo + 0.5 * hi
        cnt = count_ge(mid)
        live = done < 0.5
        collapsed = jnp.logical_or(mid <= lo, mid >= hi)
        found = jnp.logical_and(cnt == kf, jnp.logical_not(collapsed))
        fin = jnp.logical_and(live, jnp.logical_or(found, collapsed))
        thr = jnp.where(jnp.logical_and(live, found), mid, thr)
        thr = jnp.where(jnp.logical_and(live, collapsed), lo, thr)
        tie = jnp.where(jnp.logical_and(live, collapsed), 1.0, tie)
        done = jnp.where(fin, 1.0, done)
        go_up = jnp.logical_and(live, cnt > kf)
        go_dn = jnp.logical_and(live, cnt < kf)
        lo = jnp.where(go_up, mid, lo)
        hi = jnp.where(go_dn, mid, hi)
        return it + 1, pending, lo, hi, thr, done, tie

    _, _, lo, _, thr, done, tie = lax.while_loop(
        cond, body, (jnp.int32(0), jnp.int32(1), lo0, hi0, lo0, done0, zeros))
    thr = jnp.where(done > 0.5, thr, lo)
    return thr, tie


def _prompt_attn_kernel(qT_ref, qiT_ref, wT_ref, ki_ref, k_ref, vT_ref, o_ref,
                        sc_ref, acc_ref, m_ref, l_ref, st_ref, knorm_ref, *, tq, topk, seq):
    i = pl.program_id(0)
    nck = i + 1
    qpos = i * tq + lax.broadcasted_iota(jnp.int32, (1, tq), 1)
    row = lax.broadcasted_iota(jnp.int32, (tq, 1), 0)
    kf = float(topk)

    def score_chunk(c, carry):
        mx, mn = carry
        off = pl.multiple_of(c * tq, tq)
        kic = ki_ref[pl.ds(off, tq), :]
        acc = jnp.zeros((tq, tq), F32)
        for h in range(IDX_HEADS):
            d = jnp.dot(kic, qiT_ref[h * IDX_DIM:(h + 1) * IDX_DIM, :],
                        preferred_element_type=F32)
            acc = acc + jnp.maximum(d, 0.0) * (wT_ref[h:h + 1, :] * IDX_SCALE)
        valid = (off + row) <= qpos
        sc_ref[pl.ds(off, tq), :] = jnp.where(valid, acc, -jnp.inf)
        mx = jnp.maximum(mx, jnp.max(jnp.where(valid, acc, -jnp.inf), axis=0, keepdims=True))
        mn = jnp.minimum(mn, jnp.min(jnp.where(valid, acc, jnp.inf), axis=0, keepdims=True))
        return mx, mn

    mx, mn = lax.fori_loop(
        0, nck, score_chunk,
        (jnp.full((1, tq), -jnp.inf, F32), jnp.full((1, tq), jnp.inf, F32)))

    def count_where(pred):
        def body(c, acc):
            off = pl.multiple_of(c * tq, tq)
            ind = jnp.where(pred(sc_ref[pl.ds(off, tq), :], off + row), 1.0, 0.0)
            return acc + jnp.sum(ind.reshape(tq // SUBLANES, SUBLANES, tq), axis=0)
        acc = lax.fori_loop(0, nck, body, jnp.zeros((SUBLANES, tq), F32))
        return jnp.sum(acc, axis=0, keepdims=True)

    need = (qpos + 1) > topk
    thr, tie = _bisect_threshold(
        lambda t: count_where(lambda s, kp: s >= t), mn, mx, need, kf)
    thr = jnp.where(need, thr, -jnp.inf)

    any_tie = jnp.max(tie) > 0.5
    nbits = max(1, (seq - 1).bit_length())

    def tie_search(_):
        c_gt = count_where(lambda s, kp: s > thr)
        r = kf - c_gt

        def body(b, j):
            cand = j + jnp.left_shift(jnp.int32(1), nbits - 1 - b)
            cnt = count_where(lambda s, kp: jnp.logical_and(s == thr, kp < cand))
            return jnp.where(cnt < r, cand, j)
        j = lax.fori_loop(0, nbits, body, jnp.zeros((1, tq), jnp.int32))
        return jnp.where(tie > 0.5, j, jnp.int32(seq))

    jlim = lax.cond(any_tie, tie_search,
                    lambda _: jnp.full((1, tq), seq, jnp.int32), 0)
    jlim = jnp.where(need, jlim, -1)

    n_units = N_HEADS // HEADS_PER_UNIT

    @pl.when(i == 0)
    def _():
        rows = min(512, seq)
        for g in range(N_KV_HEADS):
            def body(c, mx):
                kc = k_ref[pl.ds(pl.multiple_of(c * rows, rows), rows),
                           g * HEAD_DIM:(g + 1) * HEAD_DIM].astype(F32)
                return jnp.maximum(mx, jnp.max(jnp.sum(kc * kc, axis=1, keepdims=True)))
            knorm_ref[g] = jnp.sqrt(lax.fori_loop(0, seq // rows, body, jnp.float32(0.0)))

    def unit_heads(u):
        return range(u * HEADS_PER_UNIT, (u + 1) * HEADS_PER_UNIT)

    def attention_pass(exact):
        l_ref[...] = jnp.zeros(l_ref.shape, F32)
        acc_ref[...] = jnp.zeros(acc_ref.shape, F32)
        if exact:
            m_ref[...] = jnp.full(m_ref.shape, MASKED, F32)
        else:
            for u in range(n_units):
                norms = []
                for h in unit_heads(u):
                    qh = qT_ref[h * HEAD_DIM:(h + 1) * HEAD_DIM, :].astype(F32)
                    norms.append(jnp.sqrt(jnp.sum(qh * qh, axis=0, keepdims=True))
                                 * knorm_ref[h // GROUP])
                m_ref[u] = jnp.concatenate(norms, axis=1)

        def attn_chunk(c, carry):
            off = pl.multiple_of(c * tq, tq)
            s = sc_ref[pl.ds(off, tq), :]
            kpos = off + row
            sel = jnp.logical_or(s > thr, jnp.logical_and(s == thr, kpos <= jlim))
            bias = jnp.where(sel, 0.0, MASKED)
            bias_u = jnp.concatenate([bias] * HEADS_PER_UNIT, axis=1)

            def logits(u):
                g = (u * HEADS_PER_UNIT) // GROUP
                kc = k_ref[pl.ds(off, tq), g * HEAD_DIM:(g + 1) * HEAD_DIM]
                qu = jnp.concatenate(
                    [qT_ref[h * HEAD_DIM:(h + 1) * HEAD_DIM, :] for h in unit_heads(u)], axis=1)
                return jnp.dot(kc, qu, preferred_element_type=F32)

            for u in range(min(2, n_units)):
                st_ref[u % 2] = logits(u)
            for u in range(n_units):
                g = (u * HEADS_PER_UNIT) // GROUP
                x = st_ref[u % 2] + bias_u
                if exact:
                    m_old = m_ref[u]
                    m_new = jnp.maximum(m_old, jnp.max(x, axis=0, keepdims=True))
                    alpha = jnp.exp2(m_old - m_new)
                    m_ref[u] = m_new
                else:
                    m_new = m_ref[u]
                p = jnp.exp2(x - m_new)
                psum = jnp.sum(p, axis=0, keepdims=True)
                pv = jnp.dot(vT_ref[g * HEAD_DIM:(g + 1) * HEAD_DIM, pl.ds(off, tq)],
                             p.astype(BF16), preferred_element_type=F32)
                if exact:
                    l_ref[u] = alpha * l_ref[u] + psum
                    acc_ref[u] = acc_ref[u] * alpha + pv
                else:
                    l_ref[u] = l_ref[u] + psum
                    acc_ref[u] = acc_ref[u] + pv
                if u + 2 < n_units:
                    st_ref[u % 2] = logits(u + 2)
            return carry

        lax.fori_loop(0, nck, attn_chunk, 0)

    attention_pass(exact=False)
    l_all = l_ref[...]
    in_range = jnp.logical_and(jnp.min(l_all) >= SUM_FLOOR, jnp.max(l_all) < jnp.inf)

    @pl.when(jnp.logical_not(in_range))
    def _():
        attention_pass(exact=True)

    for u in range(n_units):
        ou = acc_ref[u] * (1.0 / l_ref[u])
        for j, h in enumerate(unit_heads(u)):
            o_ref[:, h * HEAD_DIM:(h + 1) * HEAD_DIM] = (
                ou[:, j * tq:(j + 1) * tq].T.astype(o_ref.dtype))


def _prompt_attention(qT, qiT, wT, ki, kb, vT, tq):
    seq = kb.shape[0]
    topk = min(TOPK_MAX, seq // 4)
    const = lambda shape: pl.BlockSpec(shape, lambda i: (0, 0), pipeline_mode=pl.Buffered(1))
    return pl.pallas_call(
        functools.partial(_prompt_attn_kernel, tq=tq, topk=topk, seq=seq),
        grid=(seq // tq,),
        in_specs=[pl.BlockSpec((N_HEADS * HEAD_DIM, tq), lambda i: (0, i)),
                  pl.BlockSpec((IDX_HEADS * IDX_DIM, tq), lambda i: (0, i)),
                  pl.BlockSpec((IDX_HEADS, tq), lambda i: (0, i)),
                  const((seq, IDX_DIM)),
                  const((seq, N_KV_HEADS * HEAD_DIM)),
                  const((N_KV_HEADS * HEAD_DIM, seq))],
        out_specs=pl.BlockSpec((tq, N_HEADS * HEAD_DIM), lambda i: (i, 0)),
        out_shape=jax.ShapeDtypeStruct((seq, N_HEADS * HEAD_DIM), BF16),
        scratch_shapes=[pltpu.VMEM((seq, tq), F32),
                        pltpu.VMEM((N_HEADS // HEADS_PER_UNIT, HEAD_DIM, HEADS_PER_UNIT * tq), F32),
                        pltpu.VMEM((N_HEADS // HEADS_PER_UNIT, 1, HEADS_PER_UNIT * tq), F32),
                        pltpu.VMEM((N_HEADS // HEADS_PER_UNIT, 1, HEADS_PER_UNIT * tq), F32),
                        pltpu.VMEM((2, tq, HEADS_PER_UNIT * tq), F32),
                        pltpu.SMEM((N_KV_HEADS,), F32)],
        compiler_params=_cparams(1),
        name="prompt_attention",
    )(qT, qiT, wT, ki, kb, vT)


def _sample_attn_kernel(pt_ref, qi_ref, w_ref, q_ref, kin_ref, kn_ref, vn_ref, *rest,
                        n_pages, page, t_new, topk):
    ki_refs = rest[:n_pages]
    k_refs = rest[n_pages:2 * n_pages]
    v_refs = rest[2 * n_pages:3 * n_pages]
    o_ref = rest[3 * n_pages]
    sc_ref, s_ref = rest[3 * n_pages + 1:]
    del pt_ref
    past = n_pages * page
    nchunk = n_pages + 1
    total = nchunk * page
    kf = float(topk)
    nt = (((1,), (1,)), ((), ()))

    def pad_rows(x, rows):
        return jnp.concatenate(
            [x, jnp.zeros((rows - x.shape[0], x.shape[1]), x.dtype)], axis=0).astype(BF16)

    lane = lax.broadcasted_iota(jnp.int32, (1, page), 1)
    trow = lax.broadcasted_iota(jnp.int32, (t_new, 1), 0)
    new_valid = lane <= trow

    qi = qi_ref[...]
    wb = jnp.broadcast_to(w_ref[...] * IDX_SCALE, (IDX_HEADS * t_new, page))

    def head_sum(d):
        r = jnp.maximum(d, 0.0) * wb
        return jnp.sum(r.reshape(IDX_HEADS, t_new, page), axis=0)

    for p in range(n_pages):
        sc_ref[:, p * page:(p + 1) * page] = head_sum(
            jnp.dot(qi, ki_refs[p][...].astype(BF16), preferred_element_type=F32))
    s_new = head_sum(lax.dot_general(qi, pad_rows(kin_ref[...], page), nt,
                                     preferred_element_type=F32))
    sc_ref[:, past:total] = jnp.where(new_valid, s_new, -jnp.inf)

    def count_where(pred):
        acc = jnp.zeros((t_new, page), F32)
        for c in range(nchunk):
            acc = acc + jnp.where(pred(sc_ref[:, c * page:(c + 1) * page], c * page + lane),
                                  1.0, 0.0)
        return jnp.sum(acc, axis=1, keepdims=True)

    sc_all = sc_ref[...]
    mx = jnp.max(sc_all, axis=1, keepdims=True)
    mn = jnp.min(jnp.where(sc_all == -jnp.inf, jnp.inf, sc_all), axis=1, keepdims=True)
    need = (past + trow + 1) > topk
    thr, tie = _bisect_threshold(
        lambda t: count_where(lambda s, kp: s >= t), mn, mx, need, kf)
    thr = jnp.where(need, thr, -jnp.inf)

    any_tie = jnp.max(tie) > 0.5
    nbits = max(1, (total - 1).bit_length())

    def tie_search(_):
        c_gt = count_where(lambda s, kp: s > thr)
        r = kf - c_gt

        def body(b, j):
            cand = j + jnp.left_shift(jnp.int32(1), nbits - 1 - b)
            cnt = count_where(lambda s, kp: jnp.logical_and(s == thr, kp < cand))
            return jnp.where(cnt < r, cand, j)
        j = lax.fori_loop(0, nbits, body, jnp.zeros((t_new, 1), jnp.int32))
        return jnp.where(tie > 0.5, j, jnp.int32(total))

    jlim = lax.cond(any_tie, tie_search,
                    lambda _: jnp.full((t_new, 1), total, jnp.int32), 0)
    jlim = jnp.where(need, jlim, -1)

    q = q_ref[...]
    rows_g = GROUP * t_new

    def head_rows(ref, g, n_keys):
        return ref[pl.ds(g, n_keys, stride=N_KV_HEADS), :]

    def logits_chunk(c, k_of):
        s = sc_ref[:, c * page:(c + 1) * page]
        kpos = c * page + lane
        sel = jnp.logical_or(s > thr, jnp.logical_and(s == thr, kpos <= jlim))
        bias = jnp.where(sel, 0.0, MASKED)
        bias4 = jnp.concatenate([bias] * GROUP, axis=0)
        for g in range(N_KV_HEADS):
            st = lax.dot_general(q[g * rows_g:(g + 1) * rows_g, :], k_of(g), nt,
                                 preferred_element_type=F32)
            s_ref[g * rows_g:(g + 1) * rows_g, c * page:(c + 1) * page] = st + bias4

    for p in range(n_pages):
        logits_chunk(p, lambda g, p=p: head_rows(k_refs[p], g, page).astype(BF16))
    logits_chunk(n_pages, lambda g: pad_rows(head_rows(kn_ref, g, t_new), page))

    s_all = s_ref[...]
    m = jnp.max(s_all, axis=1, keepdims=True)
    pr = jnp.exp2(s_all - m)
    l = jnp.sum(pr, axis=1, keepdims=True)
    s_ref[...] = pr

    accs = [jnp.zeros((rows_g, HEAD_DIM), F32) for _ in range(N_KV_HEADS)]
    for c in range(nchunk):
        for g in range(N_KV_HEADS):
            vc = (head_rows(v_refs[c], g, page).astype(BF16) if c < n_pages
                  else pad_rows(head_rows(vn_ref, g, t_new), page))
            pc = s_ref[g * rows_g:(g + 1) * rows_g, c * page:(c + 1) * page].astype(BF16)
            accs[g] = accs[g] + jnp.dot(pc, vc, preferred_element_type=F32)

    inv = 1.0 / l
    for g in range(N_KV_HEADS):
        og = accs[g] * inv[g * rows_g:(g + 1) * rows_g, :]
        for j in range(GROUP):
            h = g * GROUP + j
            o_ref[:, h * HEAD_DIM:(h + 1) * HEAD_DIM] = (
                og[j * t_new:(j + 1) * t_new, :].astype(o_ref.dtype))


def _sample_attention(layer, page_table, qi_s, w_s, q_s, ki_new, k_new, v_new,
                      cache_ki, cache_k, cache_v):
    db, n_pages = page_table.shape
    page = cache_k.shape[2]
    t_new = k_new.shape[1]
    total = n_pages * page + t_new
    topk = min(TOPK_MAX, total // 4)
    rows = N_HEADS * t_new
    kv_rows = page * N_KV_HEADS
    new_rows = t_new * N_KV_HEADS

    def per_seq(shape):
        return pl.BlockSpec((None,) + shape, lambda b, pt: (b, 0, 0))

    def paged(shape, p):
        return pl.BlockSpec((None, None) + shape,
                            lambda b, pt, p=p: (layer, pt[b, p], 0, 0))

    in_specs = [per_seq((rows, IDX_DIM)), per_seq((rows, 1)), per_seq((rows, HEAD_DIM)),
                per_seq((t_new, IDX_DIM)), per_seq((new_rows, HEAD_DIM)),
                per_seq((new_rows, HEAD_DIM))]
    in_specs += [paged((IDX_DIM, page), p) for p in range(n_pages)]
    in_specs += [paged((kv_rows, HEAD_DIM), p) for p in range(n_pages)]
    in_specs += [paged((kv_rows, HEAD_DIM), p) for p in range(n_pages)]
    nchunk = n_pages + 1
    grid_spec = pltpu.PrefetchScalarGridSpec(
        num_scalar_prefetch=1,
        grid=(db,),
        in_specs=in_specs,
        out_specs=pl.BlockSpec((None, t_new, N_HEADS * HEAD_DIM), lambda b, pt: (b, 0, 0)),
        scratch_shapes=[pltpu.VMEM((t_new, nchunk * page), F32),
                        pltpu.VMEM((rows, nchunk * page), F32)],
    )
    ck = cache_k.reshape(cache_k.shape[0], cache_k.shape[1], kv_rows, HEAD_DIM)
    cv = cache_v.reshape(cache_v.shape[0], cache_v.shape[1], kv_rows, HEAD_DIM)
    ckit = jnp.swapaxes(cache_ki, 2, 3)
    return pl.pallas_call(
        functools.partial(_sample_attn_kernel, n_pages=n_pages, page=page,
                          t_new=t_new, topk=topk),
        grid_spec=grid_spec,
        out_shape=jax.ShapeDtypeStruct((db, t_new, N_HEADS * HEAD_DIM), F32),
        compiler_params=_cparams(1),
        name="sample_attention",
    )(page_table, qi_s, w_s, q_s, ki_new,
      k_new.reshape(db, new_rows, HEAD_DIM), v_new.reshape(db, new_rows, HEAD_DIM),
      *([ckit] * n_pages), *([ck] * n_pages), *([cv] * n_pages))


def _conv_kernel(u_ref, halo_ref, cb_ref, w_ref, o_ref, *, seq_len, tm):
    i = pl.program_id(0)
    u = u_ref[...]
    halo = halo_ref[...]
    r = lax.broadcasted_iota(jnp.int32, (tm, 1), 0)
    if seq_len >= tm:
        t = r
        halo = jnp.where(i > 0, halo, 0.0)
        h7 = halo[SUBLANES - 1:SUBLANES, :]
        h6 = halo[SUBLANES - 2:SUBLANES - 1, :]
        pre1 = jnp.broadcast_to(h7, u.shape)
        pre2 = jnp.where(r == 0, h6, h7)
    else:
        t = r % seq_len
        pre2 = halo
        pre1 = pltpu.roll(halo, tm - 1, 0)
    u1 = jnp.where(t >= 1, pltpu.roll(u, 1, 0), pre1)
    u2 = jnp.where(t >= 2, pltpu.roll(u, 2, 0), pre2)
    y = w_ref[0:1, :] * u2 + w_ref[1:2, :] * u1 + w_ref[2:3, :] * u
    o_ref[...] = (cb_ref[...] * y).astype(o_ref.dtype)


def _conv_branch(u, cb, w_conv, history, seq_len):
    t, c = u.shape
    tm = _row_tile(t)
    if history is None:
        halo_arr = u
        halo_spec = pl.BlockSpec(
            (SUBLANES, c), lambda i: (jnp.maximum(i * (tm // SUBLANES) - 1, 0), 0))
    else:
        halo_arr = history
        halo_spec = pl.BlockSpec((tm, c), lambda i: (i, 0))
    return pl.pallas_call(
        functools.partial(_conv_kernel, seq_len=seq_len, tm=tm),
        grid=(t // tm,),
        in_specs=[pl.BlockSpec((tm, c), lambda i: (i, 0)), halo_spec,
                  pl.BlockSpec((tm, c), lambda i: (i, 0)),
                  pl.BlockSpec((CONV_WIDTH, c), lambda i: (0, 0))],
        out_specs=pl.BlockSpec((tm, c), lambda i: (i, 0)),
        out_shape=jax.ShapeDtypeStruct((t, c), BF16),
        compiler_params=_cparams(1),
        name="short_conv",
    )(u, halo_arr, cb, w_conv)


def _merge_kernel(a_ref, c_ref, wa_ref, wc_ref, ga_ref, gb_ref, o_ref):
    a = jnp.dot(a_ref[...], wa_ref[...], preferred_element_type=F32)
    c = jnp.dot(c_ref[...], wc_ref[...], preferred_element_type=F32)
    o_ref[...] = (ga_ref[...] * a + gb_ref[...] * c).astype(o_ref.dtype)


def _merge(a, cpre, w_ao, w_co, gates, tn=512):
    t, d = a.shape[0], w_ao.shape[1]
    tm = _row_tile(t)
    nj = d // tn
    return pl.pallas_call(
        _merge_kernel,
        grid=(t // tm, nj),
        in_specs=[pl.BlockSpec((tm, a.shape[1]), lambda i, j: (i, 0)),
                  pl.BlockSpec((tm, cpre.shape[1]), lambda i, j: (i, 0)),
                  pl.BlockSpec((w_ao.shape[0], tn), lambda i, j: (0, j)),
                  pl.BlockSpec((w_co.shape[0], tn), lambda i, j: (0, j)),
                  pl.BlockSpec((tm, tn), lambda i, j: (i, j)),
                  pl.BlockSpec((tm, tn), lambda i, j: (i, j + nj))],
        out_specs=pl.BlockSpec((tm, tn), lambda i, j: (i, j)),
        out_shape=jax.ShapeDtypeStruct((t, d), BF16),
        compiler_params=_cparams(2),
        name="gated_merge",
    )(a, cpre, w_ao, w_co, gates, gates)


def _out_proj_kernel(m_ref, w_ref, x_ref, g_ref, x1_ref, xn_ref):
    h = jnp.dot(m_ref[...], w_ref[...], preferred_element_type=F32)
    x1 = x_ref[...] + h
    x1_ref[...] = x1
    xn_ref[...] = _rms_body(x1, g_ref[...]).astype(xn_ref.dtype)


def _out_proj(merged, w_o, x, g):
    t, d = x.shape
    tm = _row_tile(t)
    return pl.pallas_call(
        _out_proj_kernel,
        grid=(t // tm,),
        in_specs=[pl.BlockSpec((tm, d), lambda i: (i, 0)),
                  pl.BlockSpec((d, d), lambda i: (0, 0), pipeline_mode=pl.Buffered(1)),
                  pl.BlockSpec((tm, d), lambda i: (i, 0)),
                  pl.BlockSpec((1, d), lambda i: (0, 0))],
        out_specs=[pl.BlockSpec((tm, d), lambda i: (i, 0)),
                   pl.BlockSpec((tm, d), lambda i: (i, 0))],
        out_shape=[jax.ShapeDtypeStruct((t, d), F32), jax.ShapeDtypeStruct((t, d), BF16)],
        compiler_params=_cparams(1),
        name="out_proj",
    )(merged, w_o, x, g.reshape(1, d))


def _ffn_in_kernel(x_ref, wg_ref, wu_ref, o_ref):
    x = x_ref[...]
    g = jnp.dot(x, wg_ref[...], preferred_element_type=F32)
    u = jnp.dot(x, wu_ref[...], preferred_element_type=F32)
    o_ref[...] = (jax.nn.silu(g) * u).astype(o_ref.dtype)


def _ffn_in(xn, w_fi, tn=512):
    t, d = xn.shape
    d_ff = w_fi.shape[1] // 2
    tm = _row_tile(t)
    nj = d_ff // tn
    return pl.pallas_call(
        _ffn_in_kernel,
        grid=(t // tm, nj),
        in_specs=[pl.BlockSpec((tm, d), lambda i, j: (i, 0)),
                  pl.BlockSpec((d, tn), lambda i, j: (0, j)),
                  pl.BlockSpec((d, tn), lambda i, j: (0, j + nj))],
        out_specs=pl.BlockSpec((tm, tn), lambda i, j: (i, j)),
        out_shape=jax.ShapeDtypeStruct((t, d_ff), BF16),
        compiler_params=_cparams(2),
        name="ffn_in",
    )(xn, w_fi, w_fi)


def _ffn_out_kernel(a_ref, w_ref, x_ref, o_ref):
    o_ref[...] = x_ref[...] + jnp.dot(a_ref[...], w_ref[...], preferred_element_type=F32)


def _ffn_out(act, w_fo, x1, tn=512):
    t, d = x1.shape
    d_ff = act.shape[1]
    tm = _row_tile(t)
    return pl.pallas_call(
        _ffn_out_kernel,
        grid=(t // tm, d // tn),
        in_specs=[pl.BlockSpec((tm, d_ff), lambda i, j: (i, 0)),
                  pl.BlockSpec((d_ff, tn), lambda i, j: (0, j)),
                  pl.BlockSpec((tm, tn), lambda i, j: (i, j))],
        out_specs=pl.BlockSpec((tm, tn), lambda i, j: (i, j)),
        out_shape=jax.ShapeDtypeStruct((t, d), F32),
        compiler_params=_cparams(2),
        name="ffn_out",
    )(act, w_fo, x1)


def _rope_tables(pos, dim, reps):
    inv = ROPE_THETA ** (-jnp.arange(0, dim, 2, dtype=F32) / dim)
    ang = pos.astype(F32)[:, None] * inv[None, :]
    cos, sin = jnp.cos(ang), jnp.sin(ang)
    return (jnp.tile(jnp.concatenate([cos, cos], axis=1), (1, reps)),
            jnp.tile(jnp.concatenate([-sin, sin], axis=1), (1, reps)))


def _split_w_in(w, d_model):
    attn, kv, conv = N_HEADS * HEAD_DIM, N_KV_HEADS * HEAD_DIM, d_model // 2
    sizes = (attn, kv, kv, IDX_HEADS * IDX_DIM, IDX_DIM, IDX_HEADS, conv, conv, conv,
             d_model, d_model)
    parts, off = [], 0
    for n in sizes:
        parts.append(w[:, off:off + n].astype(BF16))
        off += n
    wq, wk, wv, wqi, wki, wwi, wcb, wcc, wch, wga, wgb = parts
    pad = jnp.zeros((w.shape[0], LANES - IDX_DIM - IDX_HEADS), BF16)
    return dict(q=wq, k=wk, v=wv, qi=wqi, kiwi=jnp.concatenate([wki, wwi, pad], axis=1),
                cb=wcb, cc=wcc, ch=wch, gates=jnp.concatenate([wga, wgb], axis=1))


def _layer(x, wl, tabs, attn_fn, history, seq_len):
    cos128, sin128, cos64, sin64, cos_kw, sin_kw = tabs
    xn = _rmsnorm(x, wl["norm_mix"], BF16)
    rope128 = functools.partial(_proj_rope_kernel, rope=_rope128)
    rope64 = functools.partial(_proj_rope_kernel, rope=_rope64)
    (q,) = _proj(functools.partial(rope128, scale=Q_SCALE), xn, [wl["q"]],
                 [cos128, sin128], [BF16], 512, "proj_q")
    k, kb = _proj(functools.partial(rope128, scale=1.0), xn, [wl["k"]],
                  [cos128, sin128], [F32, BF16], 512, "proj_k")
    v, vb = _proj(_proj_plain_kernel, xn, [wl["v"]], [], [F32, BF16], 512, "proj_v")
    (qi,) = _proj(functools.partial(rope64, scale=1.0), xn, [wl["qi"]],
                  [cos64, sin64], [BF16], 512, "proj_qi")
    (kiwi,) = _proj(functools.partial(rope64, scale=1.0), xn, [wl["kiwi"]],
                    [cos_kw, sin_kw], [F32], LANES, "proj_kiwi")
    cb, u = _proj(_proj_conv_kernel, xn, [wl["cb"], wl["cc"], wl["ch"]], [],
                  [F32, F32], 512, "proj_conv")
    (gates,) = _proj(_proj_gate_kernel, xn, [wl["gates"]], [], [F32], 512, "proj_gates")
    ki = kiwi[:, :IDX_DIM]
    wi = kiwi[:, IDX_DIM:IDX_DIM + IDX_HEADS]

    a = attn_fn(q=q, k=k, kb=kb, v=v, vb=vb, qi=qi, ki=ki, wi=wi)
    cpre = _conv_branch(u, cb, wl["w_conv"], history, seq_len)
    merged = _merge(a, cpre, wl["w_attn_out"], wl["w_conv_out"], gates)
    x1, xn2 = _out_proj(merged, wl["w_o"], x, wl["norm_ffn"])
    act = _ffn_in(xn2, wl["w_ffn_in"])
    x2 = _ffn_out(act, wl["w_ffn_out"], x1)
    return x2, k, v, ki, u


def kernel(x_prompt, x_sample, cache_k, cache_v, cache_ki, state_conv, page_table,
           norm_mix, w_in, w_conv, w_attn_out, w_conv_out, w_o, norm_ffn, w_ffn_in,
           w_ffn_out, norm_final):
    depth = w_in.shape[0]
    bp, seq, d_model = x_prompt.shape
    db, t_new, _ = x_sample.shape
    page = cache_k.shape[2]
    past = page_table.shape[1] * page
    conv_dim = d_model // 2
    kvd = N_KV_HEADS * HEAD_DIM

    pos_p = jnp.arange(seq)
    pos_s = jnp.tile(past + jnp.arange(t_new), db)

    def tables(pos):
        c128, s128 = _rope_tables(pos, HEAD_DIM, 1)
        c64, s64 = _rope_tables(pos, IDX_DIM, LANES // IDX_DIM)
        keep = jnp.arange(LANES)[None, :] < IDX_DIM
        return (c128, s128, c64, s64, jnp.where(keep, c64, 1.0), jnp.where(keep, s64, 0.0))

    tabs_p, tabs_s = tables(pos_p), tables(pos_s)

    layers = []
    for l in range(depth):
        wl = _split_w_in(w_in[l], d_model)
        wl.update(norm_mix=norm_mix[l], norm_ffn=norm_ffn[l], w_conv=w_conv[l],
                  w_attn_out=w_attn_out[l].astype(BF16),
                  w_conv_out=w_conv_out[l].astype(BF16), w_o=w_o[l].astype(BF16),
                  w_ffn_in=w_ffn_in[l].astype(BF16), w_ffn_out=w_ffn_out[l].astype(BF16))
        layers.append(wl)

    tq = min(256, seq)

    def prompt_attn(q, k, kb, v, vb, qi, ki, wi):
        del k, v
        return _prompt_attention(q.T, qi.T, wi.T, ki.astype(BF16), kb, vb.T, tq)

    def head_major(a, width):
        nh = a.shape[1] // width
        return a.reshape(db, t_new, nh, width).transpose(0, 2, 1, 3).reshape(
            db, nh * t_new, width)

    def sample_attn_for(l):
        def fn(q, k, kb, v, vb, qi, ki, wi):
            del kb, vb
            a = _sample_attention(
                l, page_table, head_major(qi, IDX_DIM), head_major(wi, 1),
                head_major(q, HEAD_DIM), ki.reshape(db, t_new, IDX_DIM),
                k.reshape(db, t_new, kvd), v.reshape(db, t_new, kvd),
                cache_ki, cache_k, cache_v)
            return a.reshape(db * t_new, N_HEADS * HEAD_DIM).astype(BF16)
        return fn

    def run(x, tabs, make_attn, history_for, seq_len):
        ks, vs, kis, us = [], [], [], []
        for l in range(depth):
            x, k, v, ki, u = _layer(x, layers[l], tabs, make_attn(l),
                                            history_for(l), seq_len)
            ks.append(k)
            vs.append(v)
            kis.append(ki)
            us.append(u)
        return _rmsnorm(x, norm_final, F32), ks, vs, kis, us

    assert bp == 1
    y_p, ks, vs, kis, us = run(x_prompt[0], tabs_p, lambda l: prompt_attn,
                               lambda l: None, seq)
    y_prompt = y_p[None]
    new_k_p = jnp.stack(ks).reshape(depth, bp, seq, N_KV_HEADS, HEAD_DIM)
    new_v_p = jnp.stack(vs).reshape(depth, bp, seq, N_KV_HEADS, HEAD_DIM)
    new_ki_p = jnp.stack(kis).reshape(depth, bp, seq, IDX_DIM)
    new_conv_p = jnp.stack([u[seq - (CONV_WIDTH - 1):] for u in us]).reshape(
        depth, bp, CONV_WIDTH - 1, conv_dim)

    def history_for(l):
        h = state_conv[l]
        pad = jnp.zeros((db, t_new - (CONV_WIDTH - 1), conv_dim), F32)
        return jnp.concatenate([h, pad], axis=1).reshape(db * t_new, conv_dim)

    y_s, ks, vs, kis, us = run(x_sample.reshape(db * t_new, d_model), tabs_s,
                               sample_attn_for, history_for, t_new)
    y_sample = y_s.reshape(db, t_new, d_model)
    new_k_s = jnp.stack(ks).reshape(depth, db, t_new, N_KV_HEADS, HEAD_DIM)
    new_v_s = jnp.stack(vs).reshape(depth, db, t_new, N_KV_HEADS, HEAD_DIM)
    new_ki_s = jnp.stack(kis).reshape(depth, db, t_new, IDX_DIM)
    new_conv_s = jnp.stack(
        [u.reshape(db, t_new, conv_dim)[:, t_new - (CONV_WIDTH - 1):] for u in us])

    return (y_prompt, y_sample, new_k_p, new_v_p, new_ki_p, new_conv_p,
            new_k_s, new_v_s, new_ki_s, new_conv_s)
```

```python
import functools

import jax
import jax.numpy as jnp
from jax import lax
from jax.experimental import pallas as pl
from jax.experimental.pallas import tpu as pltpu

N_HEADS = 16
N_KV_HEADS = 4
HEAD_DIM = 128
GROUP = N_HEADS // N_KV_HEADS
IDX_HEADS = 16
IDX_DIM = 64
TOPK_MAX = 256
CONV_WIDTH = 3
ROPE_THETA = 10000.0
NORM_EPS = 1e-6
IDX_SCALE = IDX_HEADS ** -0.5 * IDX_DIM ** -0.5
LOG2_E = 1.4426950408889634
Q_SCALE = HEAD_DIM ** -0.5 * LOG2_E

LANES = 128
SUBLANES = 8
VMEM_LIMIT = 48 * 1024 * 1024
MASKED = -1e30
MAX_BISECT = 200
HEADS_PER_UNIT = 2
SUM_FLOOR = 2.0 ** -100
BIG_ROW_TILE = 1024

F32 = jnp.float32
BF16 = jnp.bfloat16


def _cparams(n_grid):
    return pltpu.CompilerParams(
        dimension_semantics=("arbitrary",) * n_grid, vmem_limit_bytes=VMEM_LIMIT)


def _row_tile(t, cap=512):
    return min(cap, t)


def _layer_block(layer, block, index_map, **kw):
    return pl.BlockSpec((None,) + block, lambda *a: (layer,) + index_map(*a), **kw)


def _rms_body(x, g):
    y = x * lax.rsqrt(jnp.mean(x * x, axis=-1, keepdims=True) + NORM_EPS)
    return y * g


def _rmsnorm_kernel(x_ref, g_ref, o_ref):
    o_ref[...] = _rms_body(x_ref[...], g_ref[...]).astype(o_ref.dtype)


def _rmsnorm(x, g, out_dtype):
    t, d = x.shape
    tm = _row_tile(t)
    return pl.pallas_call(
        _rmsnorm_kernel,
        grid=(t // tm,),
        in_specs=[pl.BlockSpec((tm, d), lambda i: (i, 0)),
                  pl.BlockSpec((1, d), lambda i: (0, 0))],
        out_specs=pl.BlockSpec((tm, d), lambda i: (i, 0)),
        out_shape=jax.ShapeDtypeStruct((t, d), out_dtype),
        compiler_params=_cparams(1),
        name="rmsnorm",
    )(x, g.reshape(1, d))


def _rope128(z, cos, sin):
    outs = []
    for c in range(z.shape[1] // LANES):
        zc = z[:, c * LANES:(c + 1) * LANES]
        outs.append(zc * cos + pltpu.roll(zc, HEAD_DIM // 2, 1) * sin)
    return outs[0] if len(outs) == 1 else jnp.concatenate(outs, axis=1)


def _rope64(z, cos, sin):
    lane = lax.broadcasted_iota(jnp.int32, (1, LANES), 1)
    first_half = (lane % IDX_DIM) < (IDX_DIM // 2)
    outs = []
    for c in range(z.shape[1] // LANES):
        zc = z[:, c * LANES:(c + 1) * LANES]
        rot = jnp.where(first_half,
                        pltpu.roll(zc, LANES - IDX_DIM // 2, 1),
                        pltpu.roll(zc, IDX_DIM // 2, 1))
        outs.append(zc * cos + rot * sin)
    return outs[0] if len(outs) == 1 else jnp.concatenate(outs, axis=1)


def _store_all(z, o_refs, layouts):
    zt = z.T if "cols" in layouts else None
    for o, layout in zip(o_refs, layouts):
        if layout == "cols":
            o[...] = zt.astype(o.dtype)
        elif layout == "heads":
            n_heads = z.shape[1] // HEAD_DIM
            for g in range(n_heads):
                o[pl.ds(g, z.shape[0], stride=n_heads), :] = (
                    z[:, g * HEAD_DIM:(g + 1) * HEAD_DIM].astype(o.dtype))
        else:
            o[...] = z.astype(o.dtype)


def _proj_rope_kernel(x_ref, w_ref, cos_ref, sin_ref, *o_refs, rope, scale, layouts):
    z = jnp.dot(x_ref[...], w_ref[...], preferred_element_type=F32)
    z = rope(z, cos_ref[...], sin_ref[...])
    if scale != 1.0:
        z = z * scale
    _store_all(z, o_refs, layouts)


def _proj_plain_kernel(x_ref, w_ref, *o_refs, layouts):
    z = jnp.dot(x_ref[...], w_ref[...], preferred_element_type=F32)
    _store_all(z, o_refs, layouts)


def _proj_gate_kernel(x_ref, w_ref, o_ref):
    z = jnp.dot(x_ref[...], w_ref[...], preferred_element_type=F32)
    o_ref[...] = jax.nn.sigmoid(z).astype(o_ref.dtype)


def _proj_conv_kernel(x_ref, wb_ref, wc_ref, wh_ref, cb_ref, u_ref):
    x = x_ref[...]
    cb_ref[...] = jnp.dot(x, wb_ref[...], preferred_element_type=F32)
    cc = jnp.dot(x, wc_ref[...], preferred_element_type=F32)
    ch = jnp.dot(x, wh_ref[...], preferred_element_type=F32)
    u_ref[...] = cc * ch


def _proj(kernel, xn, ws, tables, out_dtypes, tn, name, layouts=None):
    t, k = xn.shape
    n = ws[0].shape[1]
    tm = _row_tile(t, BIG_ROW_TILE)
    tn = min(tn, n)
    if layouts is None:
        layouts = ("rows",) * len(out_dtypes)
    else:
        kernel = functools.partial(kernel, layouts=layouts)
    n_heads = n // HEAD_DIM

    def out_spec(layout):
        if layout == "cols":
            return pl.BlockSpec((tn, tm), lambda i, j: (j, i))
        if layout == "heads":
            assert tn == n
            return pl.BlockSpec((tm * n_heads, HEAD_DIM), lambda i, j: (i, 0))
        return pl.BlockSpec((tm, tn), lambda i, j: (i, j))

    def out_dims(layout):
        return {"cols": (n, t), "heads": (t * n_heads, HEAD_DIM)}.get(layout, (t, n))

    in_specs = [pl.BlockSpec((tm, k), lambda i, j: (i, 0))]
    in_specs += [pl.BlockSpec((k, tn), lambda i, j: (0, j)) for _ in ws]
    in_specs += [pl.BlockSpec((tm, LANES), lambda i, j: (i, 0)) for _ in tables]
    out_specs = [out_spec(layout) for layout in layouts]
    out_shape = [jax.ShapeDtypeStruct(out_dims(layout), dt)
                 for dt, layout in zip(out_dtypes, layouts)]
    outs = pl.pallas_call(
        kernel,
        grid=(t // tm, n // tn),
        in_specs=in_specs,
        out_specs=out_specs,
        out_shape=out_shape,
        compiler_params=_cparams(2),
        name=name,
    )(xn, *ws, *tables)
    return outs


def _bisect_threshold(count_ge, lo0, hi0, need, kf):
    big = jnp.maximum(jnp.abs(hi0) * 1e-6, 1e-30)
    hi0 = hi0 + big
    done0 = jnp.where(need, 0.0, 1.0)
    zeros = jnp.zeros_like(lo0)

    def cond(c):
        it, pending = c[0], c[1]
        return jnp.logical_and(it < MAX_BISECT, pending > 0)

    def body(c):
        it, _, lo, hi, thr, done, tie = c
        pending = (jnp.min(done) < 0.5).astype(jnp.int32)
        mid = 0.5 * lo + 0.5 * hi
        cnt = count_ge(mid)
        live = done < 0.5
        collapsed = jnp.logical_or(mid <= lo, mid >= hi)
        found = jnp.logical_and(cnt == kf, jnp.logical_not(collapsed))
        fin = jnp.logical_and(live, jnp.logical_or(found, collapsed))
        thr = jnp.where(jnp.logical_and(live, found), mid, thr)
        thr = jnp.where(jnp.logical_and(live, collapsed), lo, thr)
        tie = jnp.where(jnp.logical_and(live, collapsed), 1.0, tie)
        done = jnp.where(fin, 1.0, done)
        go_up = jnp.logical_and(live, cnt > kf)
        go_dn = jnp.logical_and(live, cnt < kf)
        lo = jnp.where(go_up, mid, lo)
        hi = jnp.where(go_dn, mid, hi)
        return it + 1, pending, lo, hi, thr, done, tie

    _, _, lo, _, thr, done, tie = lax.while_loop(
        cond, body, (jnp.int32(0), jnp.int32(1), lo0, hi0, lo0, done0, zeros))
    thr = jnp.where(done > 0.5, thr, lo)
    return thr, tie


def _prompt_attn_kernel(qT_ref, qiT_ref, wT_ref, ki_ref, k_ref, vT_ref, o_ref,
                        sc_ref, acc_ref, m_ref, l_ref, st_ref, knorm_ref, *, tq, topk, seq):
    i = pl.program_id(0)
    nck = i + 1
    qpos = i * tq + lax.broadcasted_iota(jnp.int32, (1, tq), 1)
    row = lax.broadcasted_iota(jnp.int32, (tq, 1), 0)
    kf = float(topk)

    def score_chunk(c, carry):
        mx, mn = carry
        off = pl.multiple_of(c * tq, tq)
        kic = ki_ref[pl.ds(off, tq), :]
        acc = jnp.zeros((tq, tq), F32)
        for h in range(IDX_HEADS):
            d = jnp.dot(kic, qiT_ref[h * IDX_DIM:(h + 1) * IDX_DIM, :],
                        preferred_element_type=F32)
            acc = acc + jnp.maximum(d, 0.0) * (wT_ref[h:h + 1, :] * IDX_SCALE)
        valid = (off + row) <= qpos
        sc_ref[pl.ds(off, tq), :] = jnp.where(valid, acc, -jnp.inf)
        mx = jnp.maximum(mx, jnp.max(jnp.where(valid, acc, -jnp.inf), axis=0, keepdims=True))
        mn = jnp.minimum(mn, jnp.min(jnp.where(valid, acc, jnp.inf), axis=0, keepdims=True))
        return mx, mn

    mx, mn = lax.fori_loop(
        0, nck, score_chunk,
        (jnp.full((1, tq), -jnp.inf, F32), jnp.full((1, tq), jnp.inf, F32)))

    def count_where(pred):
        def body(c, acc):
            off = pl.multiple_of(c * tq, tq)
            ind = jnp.where(pred(sc_ref[pl.ds(off, tq), :], off + row), 1.0, 0.0)
            return acc + jnp.sum(ind.reshape(tq // SUBLANES, SUBLANES, tq), axis=0)
        acc = lax.fori_loop(0, nck, body, jnp.zeros((SUBLANES, tq), F32))
        return jnp.sum(acc, axis=0, keepdims=True)

    need = (qpos + 1) > topk
    thr, tie = _bisect_threshold(
        lambda t: count_where(lambda s, kp: s >= t), mn, mx, need, kf)
    thr = jnp.where(need, thr, -jnp.inf)

    any_tie = jnp.max(tie) > 0.5
    nbits = max(1, (seq - 1).bit_length())

    def tie_search(_):
        c_gt = count_where(lambda s, kp: s > thr)
        r = kf - c_gt

        def body(b, j):
            cand = j + jnp.left_shift(jnp.int32(1), nbits - 1 - b)
            cnt = count_where(lambda s, kp: jnp.logical_and(s == thr, kp < cand))
            return jnp.where(cnt < r, cand, j)
        j = lax.fori_loop(0, nbits, body, jnp.zeros((1, tq), jnp.int32))
        return jnp.where(tie > 0.5, j, jnp.int32(seq))

    jlim = lax.cond(any_tie, tie_search,
                    lambda _: jnp.full((1, tq), seq, jnp.int32), 0)
    jlim = jnp.where(need, jlim, -1)

    n_units = N_HEADS // HEADS_PER_UNIT

    @pl.when(i == 0)
    def _():
        rows = min(512, seq)
        for g in range(N_KV_HEADS):
            def body(c, mx):
                kc = k_ref[pl.ds(pl.multiple_of(c * rows, rows), rows),
                           g * HEAD_DIM:(g + 1) * HEAD_DIM].astype(F32)
                return jnp.maximum(mx, jnp.max(jnp.sum(kc * kc, axis=1, keepdims=True)))
            knorm_ref[g] = jnp.sqrt(lax.fori_loop(0, seq // rows, body, jnp.float32(0.0)))

    def unit_heads(u):
        return range(u * HEADS_PER_UNIT, (u + 1) * HEADS_PER_UNIT)

    def attention_pass(exact):
        l_ref[...] = jnp.zeros(l_ref.shape, F32)
        acc_ref[...] = jnp.zeros(acc_ref.shape, F32)
        if exact:
            m_ref[...] = jnp.full(m_ref.shape, MASKED, F32)
        else:
            for u in range(n_units):
                norms = []
                for h in unit_heads(u):
                    qh = qT_ref[h * HEAD_DIM:(h + 1) * HEAD_DIM, :].astype(F32)
                    norms.append(jnp.sqrt(jnp.sum(qh * qh, axis=0, keepdims=True))
                                 * knorm_ref[h // GROUP])
                m_ref[u] = jnp.concatenate(norms, axis=1)

        def attn_chunk(c, carry):
            off = pl.multiple_of(c * tq, tq)
            s = sc_ref[pl.ds(off, tq), :]
            kpos = off + row
            sel = jnp.logical_or(s > thr, jnp.logical_and(s == thr, kpos <= jlim))
            bias = jnp.where(sel, 0.0, MASKED)
            bias_u = jnp.concatenate([bias] * HEADS_PER_UNIT, axis=1)

            def logits(u):
                g = (u * HEADS_PER_UNIT) // GROUP
                kc = k_ref[pl.ds(off, tq), g * HEAD_DIM:(g + 1) * HEAD_DIM]
                qu = jnp.concatenate(
                    [qT_ref[h * HEAD_DIM:(h + 1) * HEAD_DIM, :] for h in unit_heads(u)], axis=1)
                return jnp.dot(kc, qu, preferred_element_type=F32)

            for u in range(min(2, n_units)):
                st_ref[u % 2] = logits(u)
            for u in range(n_units):
                g = (u * HEADS_PER_UNIT) // GROUP
                x = st_ref[u % 2] + bias_u
                if exact:
                    m_old = m_ref[u]
                    m_new = jnp.maximum(m_old, jnp.max(x, axis=0, keepdims=True))
                    alpha = jnp.exp2(m_old - m_new)
                    m_ref[u] = m_new
                else:
                    m_new = m_ref[u]
                p = jnp.exp2(x - m_new)
                psum = jnp.sum(p, axis=0, keepdims=True)
                pv = jnp.dot(vT_ref[g * HEAD_DIM:(g + 1) * HEAD_DIM, pl.ds(off, tq)],
                             p.astype(BF16), preferred_element_type=F32)
                if exact:
                    l_ref[u] = alpha * l_ref[u] + psum
                    acc_ref[u] = acc_ref[u] * alpha + pv
                else:
                    l_ref[u] = l_ref[u] + psum
                    acc_ref[u] = acc_ref[u] + pv
                if u + 2 < n_units:
                    st_ref[u % 2] = logits(u + 2)
            return carry

        lax.fori_loop(0, nck, attn_chunk, 0)

    attention_pass(exact=False)
    l_all = l_ref[...]
    in_range = jnp.logical_and(jnp.min(l_all) >= SUM_FLOOR, jnp.max(l_all) < jnp.inf)

    @pl.when(jnp.logical_not(in_range))
    def _():
        attention_pass(exact=True)

    for u in range(n_units):
        ou = acc_ref[u] * (1.0 / l_ref[u])
        for j, h in enumerate(unit_heads(u)):
            o_ref[:, h * HEAD_DIM:(h + 1) * HEAD_DIM] = (
                ou[:, j * tq:(j + 1) * tq].T.astype(o_ref.dtype))


def _prompt_attention(qT, qiT, wT, ki, kb, vT, tq):
    seq = kb.shape[0]
    topk = min(TOPK_MAX, seq // 4)
    const = lambda shape: pl.BlockSpec(shape, lambda i: (0, 0), pipeline_mode=pl.Buffered(1))
    return pl.pallas_call(
        functools.partial(_prompt_attn_kernel, tq=tq, topk=topk, seq=seq),
        grid=(seq // tq,),
        in_specs=[pl.BlockSpec((N_HEADS * HEAD_DIM, tq), lambda i: (0, i)),
                  pl.BlockSpec((IDX_HEADS * IDX_DIM, tq), lambda i: (0, i)),
                  pl.BlockSpec((IDX_HEADS, tq), lambda i: (0, i)),
                  const((seq, IDX_DIM)),
                  const((seq, N_KV_HEADS * HEAD_DIM)),
                  const((N_KV_HEADS * HEAD_DIM, seq))],
        out_specs=pl.BlockSpec((tq, N_HEADS * HEAD_DIM), lambda i: (i, 0)),
        out_shape=jax.ShapeDtypeStruct((seq, N_HEADS * HEAD_DIM), BF16),
        scratch_shapes=[pltpu.VMEM((seq, tq), F32),
                        pltpu.VMEM((N_HEADS // HEADS_PER_UNIT, HEAD_DIM, HEADS_PER_UNIT * tq), F32),
                        pltpu.VMEM((N_HEADS // HEADS_PER_UNIT, 1, HEADS_PER_UNIT * tq), F32),
                        pltpu.VMEM((N_HEADS // HEADS_PER_UNIT, 1, HEADS_PER_UNIT * tq), F32),
                        pltpu.VMEM((2, tq, HEADS_PER_UNIT * tq), F32),
                        pltpu.SMEM((N_KV_HEADS,), F32)],
        compiler_params=_cparams(1),
        name="prompt_attention",
    )(qT, qiT, wT, ki, kb, vT)


def _sample_score_kernel(pt_ref, qi_ref, w_ref, kin_ref, *rest, n_pages, page, t_new):
    ki_refs = rest[:n_pages]
    sc_ref = rest[n_pages]
    del pt_ref
    nt = (((1,), (1,)), ((), ()))
    lane = lax.broadcasted_iota(jnp.int32, (1, page), 1)
    trow = lax.broadcasted_iota(jnp.int32, (t_new, 1), 0)
    qi = qi_ref[...]
    wb = jnp.broadcast_to(w_ref[...] * IDX_SCALE, (IDX_HEADS * t_new, page))

    def head_sum(d):
        r = jnp.maximum(d, 0.0) * wb
        return jnp.sum(r.reshape(IDX_HEADS, t_new, page), axis=0)

    for p in range(n_pages):
        sc_ref[:, p * page:(p + 1) * page] = head_sum(
            jnp.dot(qi, ki_refs[p][...].astype(BF16), preferred_element_type=F32))
    kin = jnp.concatenate(
        [kin_ref[...], jnp.zeros((page - t_new, IDX_DIM), F32)], axis=0).astype(BF16)
    s_new = head_sum(lax.dot_general(qi, kin, nt, preferred_element_type=F32))
    sc_ref[:, n_pages * page:] = jnp.where(lane <= trow, s_new, -jnp.inf)


def _select_kernel(sc_ref, thr_ref, jlim_ref, *, t_new, past, topk):
    rows, total = sc_ref.shape
    kf = float(topk)
    lane = lax.broadcasted_iota(jnp.int32, (1, total), 1)
    trow = lax.broadcasted_iota(jnp.int32, (rows, 1), 0) % t_new

    def count_where(pred):
        return jnp.sum(jnp.where(pred(sc_ref[...], lane), 1.0, 0.0), axis=1, keepdims=True)

    sc_all = sc_ref[...]
    mx = jnp.max(sc_all, axis=1, keepdims=True)
    mn = jnp.min(jnp.where(sc_all == -jnp.inf, jnp.inf, sc_all), axis=1, keepdims=True)
    need = (past + trow + 1) > topk
    thr, tie = _bisect_threshold(
        lambda t: count_where(lambda s, kp: s >= t), mn, mx, need, kf)
    thr = jnp.where(need, thr, -jnp.inf)

    any_tie = jnp.max(tie) > 0.5
    nbits = max(1, (total - 1).bit_length())

    def tie_search(_):
        c_gt = count_where(lambda s, kp: s > thr)
        r = kf - c_gt

        def body(b, j):
            cand = j + jnp.left_shift(jnp.int32(1), nbits - 1 - b)
            cnt = count_where(lambda s, kp: jnp.logical_and(s == thr, kp < cand))
            return jnp.where(cnt < r, cand, j)
        j = lax.fori_loop(0, nbits, body, jnp.zeros((rows, 1), jnp.int32))
        return jnp.where(tie > 0.5, j, jnp.int32(total))

    jlim = lax.cond(any_tie, tie_search,
                    lambda _: jnp.full((rows, 1), total, jnp.int32), 0)
    thr_ref[...] = thr
    jlim_ref[...] = jnp.where(need, jlim, -1)


def _sample_attn_kernel(pt_ref, sc_ref, thr_ref, jlim_ref, q_ref, kn_ref, vn_ref, *rest,
                        n_pages, page, t_new):
    k_refs = rest[:n_pages]
    v_refs = rest[n_pages:2 * n_pages]
    o_ref = rest[2 * n_pages]
    s_ref = rest[2 * n_pages + 1]
    del pt_ref
    nchunk = n_pages + 1
    nt = (((1,), (1,)), ((), ()))

    def pad_rows(x, rows):
        return jnp.concatenate(
            [x, jnp.zeros((rows - x.shape[0], x.shape[1]), x.dtype)], axis=0).astype(BF16)

    lane = lax.broadcasted_iota(jnp.int32, (1, page), 1)
    thr = thr_ref[...]
    jlim = jlim_ref[...]

    q = q_ref[...]
    rows_g = GROUP * t_new

    def head_rows(ref, g, n_keys):
        return ref[pl.ds(g, n_keys, stride=N_KV_HEADS), :]

    def logits_chunk(c, k_of):
        s = sc_ref[:, c * page:(c + 1) * page]
        kpos = c * page + lane
        sel = jnp.logical_or(s > thr, jnp.logical_and(s == thr, kpos <= jlim))
        bias = jnp.where(sel, 0.0, MASKED)
        bias4 = jnp.concatenate([bias] * GROUP, axis=0)
        for g in range(N_KV_HEADS):
            st = lax.dot_general(q[g * rows_g:(g + 1) * rows_g, :], k_of(g), nt,
                                 preferred_element_type=F32)
            s_ref[g * rows_g:(g + 1) * rows_g, c * page:(c + 1) * page] = st + bias4

    for p in range(n_pages):
        logits_chunk(p, lambda g, p=p: head_rows(k_refs[p], g, page).astype(BF16))
    logits_chunk(n_pages, lambda g: pad_rows(head_rows(kn_ref, g, t_new), page))

    s_all = s_ref[...]
    m = jnp.max(s_all, axis=1, keepdims=True)
    pr = jnp.exp2(s_all - m)
    l = jnp.sum(pr, axis=1, keepdims=True)
    s_ref[...] = pr

    accs = [jnp.zeros((rows_g, HEAD_DIM), F32) for _ in range(N_KV_HEADS)]
    for c in range(nchunk):
        for g in range(N_KV_HEADS):
            vc = (head_rows(v_refs[c], g, page).astype(BF16) if c < n_pages
                  else pad_rows(head_rows(vn_ref, g, t_new), page))
            pc = s_ref[g * rows_g:(g + 1) * rows_g, c * page:(c + 1) * page].astype(BF16)
            accs[g] = accs[g] + jnp.dot(pc, vc, preferred_element_type=F32)

    inv = 1.0 / l
    for g in range(N_KV_HEADS):
        og = accs[g] * inv[g * rows_g:(g + 1) * rows_g, :]
        for j in range(GROUP):
            h = g * GROUP + j
            o_ref[:, h * HEAD_DIM:(h + 1) * HEAD_DIM] = (
                og[j * t_new:(j + 1) * t_new, :].astype(o_ref.dtype))


def _sample_attention(layer, page_table, qi_s, w_s, q_s, ki_new, k_new, v_new,
                      cache_ki, cache_k, cache_v):
    db, n_pages = page_table.shape
    page = cache_k.shape[2]
    t_new = ki_new.shape[1]
    past = n_pages * page
    topk = min(TOPK_MAX, (past + t_new) // 4)
    rows = N_HEADS * t_new
    kv_rows = page * N_KV_HEADS
    new_rows = t_new * N_KV_HEADS
    total = (n_pages + 1) * page

    def per_seq(shape):
        return pl.BlockSpec((None,) + shape, lambda b, pt: (b, 0, 0))

    def paged(shape, p):
        return pl.BlockSpec((None, None) + shape,
                            lambda b, pt, p=p: (layer, pt[b, p], 0, 0))

    ck = cache_k.reshape(cache_k.shape[0], cache_k.shape[1], kv_rows, HEAD_DIM)
    cv = cache_v.reshape(cache_v.shape[0], cache_v.shape[1], kv_rows, HEAD_DIM)
    ckit = jnp.swapaxes(cache_ki, 2, 3)

    scores = pl.pallas_call(
        functools.partial(_sample_score_kernel, n_pages=n_pages, page=page, t_new=t_new),
        grid_spec=pltpu.PrefetchScalarGridSpec(
            num_scalar_prefetch=1,
            grid=(db,),
            in_specs=[per_seq((rows, IDX_DIM)), per_seq((rows, 1)), per_seq((t_new, IDX_DIM))]
            + [paged((IDX_DIM, page), p) for p in range(n_pages)],
            out_specs=per_seq((t_new, total)),
        ),
        out_shape=jax.ShapeDtypeStruct((db, t_new, total), F32),
        compiler_params=_cparams(1),
        name="sample_scores",
    )(page_table, qi_s, w_s, ki_new, *([ckit] * n_pages))

    thr, jlim = pl.pallas_call(
        functools.partial(_select_kernel, t_new=t_new, past=past, topk=topk),
        out_shape=[jax.ShapeDtypeStruct((db * t_new, 1), F32),
                   jax.ShapeDtypeStruct((db * t_new, 1), jnp.int32)],
        compiler_params=pltpu.CompilerParams(vmem_limit_bytes=VMEM_LIMIT),
        name="sample_select",
    )(scores.reshape(db * t_new, total))

    return pl.pallas_call(
        functools.partial(_sample_attn_kernel, n_pages=n_pages, page=page, t_new=t_new),
        grid_spec=pltpu.PrefetchScalarGridSpec(
            num_scalar_prefetch=1,
            grid=(db,),
            in_specs=[per_seq((t_new, total)), per_seq((t_new, 1)), per_seq((t_new, 1)),
                      per_seq((rows, HEAD_DIM)), per_seq((new_rows, HEAD_DIM)),
                      per_seq((new_rows, HEAD_DIM))]
            + [paged((kv_rows, HEAD_DIM), p) for p in range(n_pages)]
            + [paged((kv_rows, HEAD_DIM), p) for p in range(n_pages)],
            out_specs=pl.BlockSpec((None, t_new, N_HEADS * HEAD_DIM), lambda b, pt: (b, 0, 0)),
            scratch_shapes=[pltpu.VMEM((rows, total), F32)],
        ),
        out_shape=jax.ShapeDtypeStruct((db, t_new, N_HEADS * HEAD_DIM), F32),
        compiler_params=_cparams(1),
        name="sample_attention",
    )(page_table, scores, thr.reshape(db, t_new, 1), jlim.reshape(db, t_new, 1), q_s,
      k_new, v_new, *([ck] * n_pages), *([cv] * n_pages))


def _conv_kernel(u_ref, halo_ref, cb_ref, w_ref, o_ref, *, seq_len, tm):
    i = pl.program_id(0)
    u = u_ref[...]
    halo = halo_ref[...]
    r = lax.broadcasted_iota(jnp.int32, (tm, 1), 0)
    if seq_len >= tm:
        t = r
        halo = jnp.where(i > 0, halo, 0.0)
        h7 = halo[SUBLANES - 1:SUBLANES, :]
        h6 = halo[SUBLANES - 2:SUBLANES - 1, :]
        pre1 = jnp.broadcast_to(h7, u.shape)
        pre2 = jnp.where(r == 0, h6, h7)
    else:
        t = r % seq_len
        pre2 = halo
        pre1 = pltpu.roll(halo, tm - 1, 0)
    u1 = jnp.where(t >= 1, pltpu.roll(u, 1, 0), pre1)
    u2 = jnp.where(t >= 2, pltpu.roll(u, 2, 0), pre2)
    y = w_ref[0:1, :] * u2 + w_ref[1:2, :] * u1 + w_ref[2:3, :] * u
    o_ref[...] = (cb_ref[...] * y).astype(o_ref.dtype)


def _conv_branch(u, cb, w_conv, history, seq_len):
    t, c = u.shape
    tm = _row_tile(t)
    if history is None:
        halo_arr = u
        halo_spec = pl.BlockSpec(
            (SUBLANES, c), lambda i: (jnp.maximum(i * (tm // SUBLANES) - 1, 0), 0))
    else:
        halo_arr = history
        halo_spec = pl.BlockSpec((tm, c), lambda i: (i, 0))
    return pl.pallas_call(
        functools.partial(_conv_kernel, seq_len=seq_len, tm=tm),
        grid=(t // tm,),
        in_specs=[pl.BlockSpec((tm, c), lambda i: (i, 0)), halo_spec,
                  pl.BlockSpec((tm, c), lambda i: (i, 0)),
                  pl.BlockSpec((CONV_WIDTH, c), lambda i: (0, 0))],
        out_specs=pl.BlockSpec((tm, c), lambda i: (i, 0)),
        out_shape=jax.ShapeDtypeStruct((t, c), BF16),
        compiler_params=_cparams(1),
        name="short_conv",
    )(u, halo_arr, cb, w_conv)


def _merge_kernel(a_ref, c_ref, wa_ref, wc_ref, ga_ref, gb_ref, o_ref):
    a = jnp.dot(a_ref[...], wa_ref[...], preferred_element_type=F32)
    c = jnp.dot(c_ref[...], wc_ref[...], preferred_element_type=F32)
    o_ref[...] = (ga_ref[...] * a + gb_ref[...] * c).astype(o_ref.dtype)


def _merge(a, cpre, w_ao, w_co, layer, gates, tn=512):
    t, d = a.shape[0], w_ao.shape[2]
    tm = _row_tile(t, BIG_ROW_TILE)
    nj = d // tn
    return pl.pallas_call(
        _merge_kernel,
        grid=(t // tm, nj),
        in_specs=[pl.BlockSpec((tm, a.shape[1]), lambda i, j: (i, 0)),
                  pl.BlockSpec((tm, cpre.shape[1]), lambda i, j: (i, 0)),
                  _layer_block(layer, (w_ao.shape[1], tn), lambda i, j: (0, j)),
                  _layer_block(layer, (w_co.shape[1], tn), lambda i, j: (0, j)),
                  pl.BlockSpec((tm, tn), lambda i, j: (i, j)),
                  pl.BlockSpec((tm, tn), lambda i, j: (i, j + nj))],
        out_specs=pl.BlockSpec((tm, tn), lambda i, j: (i, j)),
        out_shape=jax.ShapeDtypeStruct((t, d), BF16),
        compiler_params=_cparams(2),
        name="gated_merge",
    )(a, cpre, w_ao, w_co, gates, gates)


def _out_proj_kernel(m_ref, w_ref, x_ref, g_ref, x1_ref, xn_ref):
    h = jnp.dot(m_ref[...], w_ref[...], preferred_element_type=F32)
    x1 = x_ref[...] + h
    x1_ref[...] = x1
    xn_ref[...] = _rms_body(x1, g_ref[...]).astype(xn_ref.dtype)


def _out_proj(merged, w_o, layer, x, g):
    t, d = x.shape
    tm = _row_tile(t)
    return pl.pallas_call(
        _out_proj_kernel,
        grid=(t // tm,),
        in_specs=[pl.BlockSpec((tm, d), lambda i: (i, 0)),
                  _layer_block(layer, (d, d), lambda i: (0, 0), pipeline_mode=pl.Buffered(1)),
                  pl.BlockSpec((tm, d), lambda i: (i, 0)),
                  pl.BlockSpec((1, d), lambda i: (0, 0))],
        out_specs=[pl.BlockSpec((tm, d), lambda i: (i, 0)),
                   pl.BlockSpec((tm, d), lambda i: (i, 0))],
        out_shape=[jax.ShapeDtypeStruct((t, d), F32), jax.ShapeDtypeStruct((t, d), BF16)],
        compiler_params=_cparams(1),
        name="out_proj",
    )(merged, w_o, x, g.reshape(1, d))


def _ffn_in_kernel(x_ref, wg_ref, wu_ref, o_ref):
    x = x_ref[...]
    g = jnp.dot(x, wg_ref[...], preferred_element_type=F32)
    u = jnp.dot(x, wu_ref[...], preferred_element_type=F32)
    o_ref[...] = (jax.nn.silu(g) * u).astype(o_ref.dtype)


def _ffn_in(xn, w_fi, layer, tn=512):
    t, d = xn.shape
    d_ff = w_fi.shape[2] // 2
    tm = _row_tile(t, BIG_ROW_TILE)
    nj = d_ff // tn
    return pl.pallas_call(
        _ffn_in_kernel,
        grid=(t // tm, nj),
        in_specs=[pl.BlockSpec((tm, d), lambda i, j: (i, 0)),
                  _layer_block(layer, (d, tn), lambda i, j: (0, j)),
                  _layer_block(layer, (d, tn), lambda i, j: (0, j + nj))],
        out_specs=pl.BlockSpec((tm, tn), lambda i, j: (i, j)),
        out_shape=jax.ShapeDtypeStruct((t, d_ff), BF16),
        compiler_params=_cparams(2),
        name="ffn_in",
    )(xn, w_fi, w_fi)


def _ffn_out_kernel(a_ref, w_ref, x_ref, o_ref):
    o_ref[...] = x_ref[...] + jnp.dot(a_ref[...], w_ref[...], preferred_element_type=F32)


def _ffn_out(act, w_fo, layer, x1, tn=512):
    t, d = x1.shape
    d_ff = act.shape[1]
    tm = _row_tile(t)
    return pl.pallas_call(
        _ffn_out_kernel,
        grid=(t // tm, d // tn),
        in_specs=[pl.BlockSpec((tm, d_ff), lambda i, j: (i, 0)),
                  _layer_block(layer, (d_ff, tn), lambda i, j: (0, j)),
                  pl.BlockSpec((tm, tn), lambda i, j: (i, j))],
        out_specs=pl.BlockSpec((tm, tn), lambda i, j: (i, j)),
        out_shape=jax.ShapeDtypeStruct((t, d), F32),
        compiler_params=_cparams(2),
        name="ffn_out",
    )(act, w_fo, x1)


def _rope_tables(pos, dim, reps):
    inv = ROPE_THETA ** (-jnp.arange(0, dim, 2, dtype=F32) / dim)
    ang = pos.astype(F32)[:, None] * inv[None, :]
    cos, sin = jnp.cos(ang), jnp.sin(ang)
    return (jnp.tile(jnp.concatenate([cos, cos], axis=1), (1, reps)),
            jnp.tile(jnp.concatenate([-sin, sin], axis=1), (1, reps)))


def _split_w_in(w, d_model):
    attn, kv, conv = N_HEADS * HEAD_DIM, N_KV_HEADS * HEAD_DIM, d_model // 2
    sizes = (attn, kv, kv, IDX_HEADS * IDX_DIM, IDX_DIM, IDX_HEADS, conv, conv, conv,
             d_model, d_model)
    parts, off = [], 0
    for n in sizes:
        parts.append(w[:, off:off + n].astype(BF16))
        off += n
    wq, wk, wv, wqi, wki, wwi, wcb, wcc, wch, wga, wgb = parts
    pad = jnp.zeros((w.shape[0], LANES - IDX_DIM - IDX_HEADS), BF16)
    return dict(q=wq, k=wk, v=wv, qi=wqi, kiwi=jnp.concatenate([wki, wwi, pad], axis=1),
                cb=wcb, cc=wcc, ch=wch, gates=jnp.concatenate([wga, wgb], axis=1))


def _layer(x, wl, tabs, attn_fn, history, seq_len, attn_cols):
    cos128, sin128, cos64, sin64, cos_kw, sin_kw = tabs
    xn = _rmsnorm(x, wl["norm_mix"], BF16)
    rope128 = functools.partial(_proj_rope_kernel, rope=_rope128)
    rope64 = functools.partial(_proj_rope_kernel, rope=_rope64)
    qlay = "cols" if attn_cols else "rows"
    (q,) = _proj(functools.partial(rope128, scale=Q_SCALE), xn, [wl["q"]],
                 [cos128, sin128], [BF16], 512, "proj_q", (qlay,))
    k, kb = _proj(functools.partial(rope128, scale=1.0), xn, [wl["k"]],
                  [cos128, sin128], [F32, BF16], 512, "proj_k", ("heads", "rows"))
    v, vb = _proj(_proj_plain_kernel, xn, [wl["v"]], [], [F32, BF16], 512, "proj_v",
                  ("heads", qlay))
    (qi,) = _proj(functools.partial(rope64, scale=1.0), xn, [wl["qi"]],
                  [cos64, sin64], [BF16], 512, "proj_qi", (qlay,))
    (kiwi,) = _proj(functools.partial(rope64, scale=1.0), xn, [wl["kiwi"]],
                    [cos_kw, sin_kw], [F32], LANES, "proj_kiwi", ("rows",))
    cb, u = _proj(_proj_conv_kernel, xn, [wl["cb"], wl["cc"], wl["ch"]], [],
                  [F32, F32], 512, "proj_conv")
    (gates,) = _proj(_proj_gate_kernel, xn, [wl["gates"]], [], [BF16], 512, "proj_gates")
    ki = kiwi[:, :IDX_DIM]
    wi = kiwi[:, IDX_DIM:IDX_DIM + IDX_HEADS]

    a = attn_fn(q=q, k=k, kb=kb, v=v, vb=vb, qi=qi, ki=ki, wi=wi)
    cpre = _conv_branch(u, cb, wl["w_conv"], history, seq_len)
    l = wl["layer"]
    merged = _merge(a, cpre, wl["w_attn_out"], wl["w_conv_out"], l, gates)
    x1, xn2 = _out_proj(merged, wl["w_o"], l, x, wl["norm_ffn"])
    act = _ffn_in(xn2, wl["w_ffn_in"], l)
    x2 = _ffn_out(act, wl["w_ffn_out"], l, x1)
    return x2, k, v, ki, u


def kernel(x_prompt, x_sample, cache_k, cache_v, cache_ki, state_conv, page_table,
           norm_mix, w_in, w_conv, w_attn_out, w_conv_out, w_o, norm_ffn, w_ffn_in,
           w_ffn_out, norm_final):
    depth = w_in.shape[0]
    bp, seq, d_model = x_prompt.shape
    db, t_new, _ = x_sample.shape
    page = cache_k.shape[2]
    past = page_table.shape[1] * page
    conv_dim = d_model // 2

    pos_p = jnp.arange(seq)
    pos_s = jnp.tile(past + jnp.arange(t_new), db)

    def tables(pos):
        c128, s128 = _rope_tables(pos, HEAD_DIM, 1)
        c64, s64 = _rope_tables(pos, IDX_DIM, LANES // IDX_DIM)
        keep = jnp.arange(LANES)[None, :] < IDX_DIM
        return (c128, s128, c64, s64, jnp.where(keep, c64, 1.0), jnp.where(keep, s64, 0.0))

    tabs_p, tabs_s = tables(pos_p), tables(pos_s)

    stacked = dict(w_attn_out=w_attn_out.astype(BF16), w_conv_out=w_conv_out.astype(BF16),
                   w_o=w_o.astype(BF16), w_ffn_in=w_ffn_in.astype(BF16),
                   w_ffn_out=w_ffn_out.astype(BF16))
    layers = []
    for l in range(depth):
        wl = _split_w_in(w_in[l], d_model)
        wl.update(layer=l, norm_mix=norm_mix[l], norm_ffn=norm_ffn[l], w_conv=w_conv[l],
                  **stacked)
        layers.append(wl)

    tq = min(256, seq)

    def prompt_attn(q, k, kb, v, vb, qi, ki, wi):
        del k, v
        return _prompt_attention(q, qi, wi.T, ki.astype(BF16), kb, vb, tq)

    def head_major(a, width):
        nh = a.shape[1] // width
        return a.reshape(db, t_new, nh, width).transpose(0, 2, 1, 3).reshape(
            db, nh * t_new, width)

    def sample_attn_for(l):
        def fn(q, k, kb, v, vb, qi, ki, wi):
            del kb, vb
            a = _sample_attention(
                l, page_table, head_major(qi, IDX_DIM), head_major(wi, 1),
                head_major(q, HEAD_DIM), ki.reshape(db, t_new, IDX_DIM),
                k.reshape(db, t_new * N_KV_HEADS, HEAD_DIM),
                v.reshape(db, t_new * N_KV_HEADS, HEAD_DIM),
                cache_ki, cache_k, cache_v)
            return a.reshape(db * t_new, N_HEADS * HEAD_DIM).astype(BF16)
        return fn

    def run(x, tabs, make_attn, history_for, seq_len, attn_cols):
        ks, vs, kis, us = [], [], [], []
        for l in range(depth):
            x, k, v, ki, u = _layer(x, layers[l], tabs, make_attn(l),
                                    history_for(l), seq_len, attn_cols)
            ks.append(k)
            vs.append(v)
            kis.append(ki)
            us.append(u)
        return _rmsnorm(x, norm_final, F32), ks, vs, kis, us

    assert bp == 1
    y_p, ks, vs, kis, us = run(x_prompt[0], tabs_p, lambda l: prompt_attn,
                               lambda l: None, seq, True)
    y_prompt = y_p[None]
    new_k_p = jnp.stack(ks).reshape(depth, bp, seq, N_KV_HEADS, HEAD_DIM)
    new_v_p = jnp.stack(vs).reshape(depth, bp, seq, N_KV_HEADS, HEAD_DIM)
    new_ki_p = jnp.stack(kis).reshape(depth, bp, seq, IDX_DIM)
    new_conv_p = jnp.stack([u[seq - (CONV_WIDTH - 1):] for u in us]).reshape(
        depth, bp, CONV_WIDTH - 1, conv_dim)

    def history_for(l):
        h = state_conv[l]
        pad = jnp.zeros((db, t_new - (CONV_WIDTH - 1), conv_dim), F32)
        return jnp.concatenate([h, pad], axis=1).reshape(db * t_new, conv_dim)

    y_s, ks, vs, kis, us = run(x_sample.reshape(db * t_new, d_model), tabs_s,
                               sample_attn_for, history_for, t_new, False)
    y_sample = y_s.reshape(db, t_new, d_model)
    new_k_s = jnp.stack(ks).reshape(depth, db, t_new, N_KV_HEADS, HEAD_DIM)
    new_v_s = jnp.stack(vs).reshape(depth, db, t_new, N_KV_HEADS, HEAD_DIM)
    new_ki_s = jnp.stack(kis).reshape(depth, db, t_new, IDX_DIM)
    new_conv_s = jnp.stack(
        [u.reshape(db, t_new, conv_dim)[:, t_new - (CONV_WIDTH - 1):] for u in us])

    return (y_prompt, y_sample, new_k_p, new_v_p, new_ki_p, new_conv_p,
            new_k_s, new_v_s, new_ki_s, new_conv_s)
```

```python
import functools

import jax
import jax.numpy as jnp
from jax import lax
from jax.experimental import pallas as pl
from jax.experimental.pallas import tpu as pltpu

N_HEADS = 16
N_KV_HEADS = 4
HEAD_DIM = 128
GROUP = N_HEADS // N_KV_HEADS
IDX_HEADS = 16
IDX_DIM = 64
TOPK_MAX = 256
CONV_WIDTH = 3
ROPE_THETA = 10000.0
NORM_EPS = 1e-6
IDX_SCALE = IDX_HEADS ** -0.5 * IDX_DIM ** -0.5
LOG2_E = 1.4426950408889634
Q_SCALE = HEAD_DIM ** -0.5 * LOG2_E

LANES = 128
SUBLANES = 8
VMEM_LIMIT = 48 * 1024 * 1024
MASKED = -1e30
MAX_BISECT = 200
HEADS_PER_UNIT = 4
SUM_FLOOR = 2.0 ** -100
COUNT_ROWS = 32
BIG_ROW_TILE = 1024

F32 = jnp.float32
BF16 = jnp.bfloat16


def _cparams(n_grid):
    return pltpu.CompilerParams(
        dimension_semantics=("arbitrary",) * n_grid, vmem_limit_bytes=VMEM_LIMIT)


def _row_tile(t, cap=512):
    return min(cap, t)


def _layer_block(layer, block, index_map, **kw):
    return pl.BlockSpec((None,) + block, lambda *a: (layer,) + index_map(*a), **kw)


def _rms_body(x, g):
    y = x * lax.rsqrt(jnp.mean(x * x, axis=-1, keepdims=True) + NORM_EPS)
    return y * g


def _rmsnorm_kernel(x_ref, g_ref, o_ref):
    o_ref[...] = _rms_body(x_ref[...], g_ref[...]).astype(o_ref.dtype)


def _rmsnorm(x, g, out_dtype):
    t, d = x.shape
    tm = _row_tile(t)
    return pl.pallas_call(
        _rmsnorm_kernel,
        grid=(t // tm,),
        in_specs=[pl.BlockSpec((tm, d), lambda i: (i, 0)),
                  pl.BlockSpec((1, d), lambda i: (0, 0))],
        out_specs=pl.BlockSpec((tm, d), lambda i: (i, 0)),
        out_shape=jax.ShapeDtypeStruct((t, d), out_dtype),
        compiler_params=_cparams(1),
        name="rmsnorm",
    )(x, g.reshape(1, d))


def _rope128(z, cos, sin):
    outs = []
    for c in range(z.shape[1] // LANES):
        zc = z[:, c * LANES:(c + 1) * LANES]
        outs.append(zc * cos + pltpu.roll(zc, HEAD_DIM // 2, 1) * sin)
    return outs[0] if len(outs) == 1 else jnp.concatenate(outs, axis=1)


def _rope64(z, cos, sin):
    lane = lax.broadcasted_iota(jnp.int32, (1, LANES), 1)
    first_half = (lane % IDX_DIM) < (IDX_DIM // 2)
    outs = []
    for c in range(z.shape[1] // LANES):
        zc = z[:, c * LANES:(c + 1) * LANES]
        rot = jnp.where(first_half,
                        pltpu.roll(zc, LANES - IDX_DIM // 2, 1),
                        pltpu.roll(zc, IDX_DIM // 2, 1))
        outs.append(zc * cos + rot * sin)
    return outs[0] if len(outs) == 1 else jnp.concatenate(outs, axis=1)


def _rope_cols(zt, cos_t, sin_t, width):
    half = width // 2
    outs = []
    for c in range(zt.shape[0] // LANES):
        blk = zt[c * LANES:(c + 1) * LANES, :]
        swapped = jnp.concatenate(
            [blk[b + half:b + width, :] if first else blk[b:b + half, :]
             for b in range(0, LANES, width) for first in (True, False)], axis=0)
        outs.append(blk * cos_t + swapped * sin_t)
    return outs[0] if len(outs) == 1 else jnp.concatenate(outs, axis=0)


def _store_all(z, o_refs, layouts):
    zt = z.T if "cols" in layouts else None
    for o, layout in zip(o_refs, layouts):
        if layout == "cols":
            o[...] = zt.astype(o.dtype)
        elif layout == "heads":
            n_heads = z.shape[1] // HEAD_DIM
            for g in range(n_heads):
                o[pl.ds(g, z.shape[0], stride=n_heads), :] = (
                    z[:, g * HEAD_DIM:(g + 1) * HEAD_DIM].astype(o.dtype))
        else:
            o[...] = z.astype(o.dtype)


def _proj_rope_kernel(x_ref, w_ref, cos_ref, sin_ref, *o_refs, rope, scale, layouts):
    z = jnp.dot(x_ref[...], w_ref[...], preferred_element_type=F32)
    if scale != 1.0:
        z = z * scale
    if layouts == ("cols",):
        width = HEAD_DIM if rope is _rope128 else IDX_DIM
        zt = _rope_cols(z.T, cos_ref[...], sin_ref[...], width)
        o_refs[0][...] = zt.astype(o_refs[0].dtype)
    else:
        _store_all(rope(z, cos_ref[...], sin_ref[...]), o_refs, layouts)


def _proj_plain_kernel(x_ref, w_ref, *o_refs, layouts):
    z = jnp.dot(x_ref[...], w_ref[...], preferred_element_type=F32)
    _store_all(z, o_refs, layouts)


def _proj_gate_kernel(x_ref, w_ref, o_ref):
    z = jnp.dot(x_ref[...], w_ref[...], preferred_element_type=F32)
    o_ref[...] = jax.nn.sigmoid(z).astype(o_ref.dtype)


def _proj_conv_kernel(x_ref, wb_ref, wc_ref, wh_ref, cb_ref, u_ref):
    x = x_ref[...]
    cb_ref[...] = jnp.dot(x, wb_ref[...], preferred_element_type=F32)
    cc = jnp.dot(x, wc_ref[...], preferred_element_type=F32)
    ch = jnp.dot(x, wh_ref[...], preferred_element_type=F32)
    u_ref[...] = cc * ch


def _proj(kernel, xn, ws, tables, out_dtypes, tn, name, layouts=None):
    t, k = xn.shape
    n = ws[0].shape[1]
    tm = _row_tile(t, BIG_ROW_TILE)
    tn = min(tn, n)
    if layouts is None:
        layouts = ("rows",) * len(out_dtypes)
    else:
        kernel = functools.partial(kernel, layouts=layouts)
    n_heads = n // HEAD_DIM

    def out_spec(layout):
        if layout == "cols":
            return pl.BlockSpec((tn, tm), lambda i, j: (j, i))
        if layout == "heads":
            assert tn == n
            return pl.BlockSpec((tm * n_heads, HEAD_DIM), lambda i, j: (i, 0))
        return pl.BlockSpec((tm, tn), lambda i, j: (i, j))

    def out_dims(layout):
        return {"cols": (n, t), "heads": (t * n_heads, HEAD_DIM)}.get(layout, (t, n))

    in_specs = [pl.BlockSpec((tm, k), lambda i, j: (i, 0))]
    in_specs += [pl.BlockSpec((k, tn), lambda i, j: (0, j)) for _ in ws]
    if tables and layouts == ("cols",):
        tables = [tb.T for tb in tables]
        in_specs += [pl.BlockSpec((LANES, tm), lambda i, j: (0, i)) for _ in tables]
    else:
        in_specs += [pl.BlockSpec((tm, LANES), lambda i, j: (i, 0)) for _ in tables]
    out_specs = [out_spec(layout) for layout in layouts]
    out_shape = [jax.ShapeDtypeStruct(out_dims(layout), dt)
                 for dt, layout in zip(out_dtypes, layouts)]
    outs = pl.pallas_call(
        kernel,
        grid=(t // tm, n // tn),
        in_specs=in_specs,
        out_specs=out_specs,
        out_shape=out_shape,
        compiler_params=_cparams(2),
        name=name,
    )(xn, *ws, *tables)
    return outs


def _bisect_threshold(count_ge, lo0, hi0, need, kf):
    big = jnp.maximum(jnp.abs(hi0) * 1e-6, 1e-30)
    hi0 = hi0 + big
    done0 = jnp.where(need, 0.0, 1.0)
    zeros = jnp.zeros_like(lo0)

    def cond(c):
        it, pending = c[0], c[1]
        return jnp.logical_and(it < MAX_BISECT, pending > 0)

    def body(c):
        it, _, lo, hi, thr, done, tie = c
        pending = (jnp.min(done) < 0.5).astype(jnp.int32)
        mid = 0.5 * lo + 0.5 * hi
        cnt = count_ge(mid)
        live = done < 0.5
        collapsed = jnp.logical_or(mid <= lo, mid >= hi)
        found = jnp.logical_and(cnt == kf, jnp.logical_not(collapsed))
        fin = jnp.logical_and(live, jnp.logical_or(found, collapsed))
        thr = jnp.where(jnp.logical_and(live, found), mid, thr)
        thr = jnp.where(jnp.logical_and(live, collapsed), lo, thr)
        tie = jnp.where(jnp.logical_and(live, collapsed), 1.0, tie)
        done = jnp.where(fin, 1.0, done)
        go_up = jnp.logical_and(live, cnt > kf)
        go_dn = jnp.logical_and(live, cnt < kf)
        lo = jnp.where(go_up, mid, lo)
        hi = jnp.where(go_dn, mid, hi)
        return it + 1, pending, lo, hi, thr, done, tie

    _, _, lo, _, thr, done, tie = lax.while_loop(
        cond, body, (jnp.int32(0), jnp.int32(1), lo0, hi0, lo0, done0, zeros))
    thr = jnp.where(done > 0.5, thr, lo)
    return thr, tie


def _prompt_attn_kernel(qT_ref, qiT_ref, wT_ref, ki_ref, k_ref, vT_ref, o_ref,
                        sc_ref, acc_ref, m_ref, l_ref, st_ref, knorm_ref, *, tq, topk, seq):
    i = pl.program_id(0)
    nck = i + 1
    qpos = i * tq + lax.broadcasted_iota(jnp.int32, (1, tq), 1)
    row = lax.broadcasted_iota(jnp.int32, (tq, 1), 0)
    kf = float(topk)

    def score_chunk(c, carry):
        mx, mn = carry
        off = pl.multiple_of(c * tq, tq)
        kic = ki_ref[pl.ds(off, tq), :]
        acc = jnp.zeros((tq, tq), F32)
        for h in range(IDX_HEADS):
            d = jnp.dot(kic, qiT_ref[h * IDX_DIM:(h + 1) * IDX_DIM, :],
                        preferred_element_type=F32)
            acc = acc + jnp.maximum(d, 0.0) * (wT_ref[h:h + 1, :] * IDX_SCALE)
        valid = (off + row) <= qpos
        sc_ref[pl.ds(off, tq), :] = jnp.where(valid, acc, -jnp.inf)
        mx = jnp.maximum(mx, jnp.max(jnp.where(valid, acc, -jnp.inf), axis=0, keepdims=True))
        mn = jnp.minimum(mn, jnp.min(jnp.where(valid, acc, jnp.inf), axis=0, keepdims=True))
        return mx, mn

    mx, mn = lax.fori_loop(
        0, nck, score_chunk,
        (jnp.full((1, tq), -jnp.inf, F32), jnp.full((1, tq), jnp.inf, F32)))

    def count_where(pred):
        def body(c, acc):
            off = pl.multiple_of(c * tq, tq)
            ind = jnp.where(pred(sc_ref[pl.ds(off, tq), :], off + row), 1.0, 0.0)
            return acc + jnp.sum(ind.reshape(tq // COUNT_ROWS, COUNT_ROWS, tq), axis=0)
        acc = lax.fori_loop(0, nck, body, jnp.zeros((COUNT_ROWS, tq), F32))
        return jnp.sum(acc, axis=0, keepdims=True)

    need = (qpos + 1) > topk
    thr, tie = _bisect_threshold(
        lambda t: count_where(lambda s, kp: s >= t), mn, mx, need, kf)
    thr = jnp.where(need, thr, -jnp.inf)

    any_tie = jnp.max(tie) > 0.5
    nbits = max(1, (seq - 1).bit_length())

    def tie_search(_):
        c_gt = count_where(lambda s, kp: s > thr)
        r = kf - c_gt

        def body(b, j):
            cand = j + jnp.left_shift(jnp.int32(1), nbits - 1 - b)
            cnt = count_where(lambda s, kp: jnp.logical_and(s == thr, kp < cand))
            return jnp.where(cnt < r, cand, j)
        j = lax.fori_loop(0, nbits, body, jnp.zeros((1, tq), jnp.int32))
        return jnp.where(tie > 0.5, j, jnp.int32(seq))

    jlim = lax.cond(any_tie, tie_search,
                    lambda _: jnp.full((1, tq), seq, jnp.int32), 0)
    jlim = jnp.where(need, jlim, -1)

    n_units = N_HEADS // HEADS_PER_UNIT

    @pl.when(i == 0)
    def _():
        rows = min(512, seq)
        for g in range(N_KV_HEADS):
            def body(c, mx):
                kc = k_ref[pl.ds(pl.multiple_of(c * rows, rows), rows),
                           g * HEAD_DIM:(g + 1) * HEAD_DIM].astype(F32)
                return jnp.maximum(mx, jnp.max(jnp.sum(kc * kc, axis=1, keepdims=True)))
            knorm_ref[g] = jnp.sqrt(lax.fori_loop(0, seq // rows, body, jnp.float32(0.0)))

    def unit_heads(u):
        return range(u * HEADS_PER_UNIT, (u + 1) * HEADS_PER_UNIT)

    def attention_pass(exact):
        l_ref[...] = jnp.zeros(l_ref.shape, F32)
        acc_ref[...] = jnp.zeros(acc_ref.shape, F32)
        if exact:
            m_ref[...] = jnp.full(m_ref.shape, MASKED, F32)
        else:
            for u in range(n_units):
                norms = []
                for h in unit_heads(u):
                    qh = qT_ref[h * HEAD_DIM:(h + 1) * HEAD_DIM, :].astype(F32)
                    norms.append(jnp.sqrt(jnp.sum(qh * qh, axis=0, keepdims=True))
                                 * knorm_ref[h // GROUP])
                m_ref[u] = jnp.concatenate(norms, axis=1)

        def attn_chunk(c, carry):
            off = pl.multiple_of(c * tq, tq)
            s = sc_ref[pl.ds(off, tq), :]
            kpos = off + row
            sel = jnp.logical_or(s > thr, jnp.logical_and(s == thr, kpos <= jlim))
            bias = jnp.where(sel, 0.0, MASKED)
            bias_u = jnp.concatenate([bias] * HEADS_PER_UNIT, axis=1)

            def logits(u):
                g = (u * HEADS_PER_UNIT) // GROUP
                kc = k_ref[pl.ds(off, tq), g * HEAD_DIM:(g + 1) * HEAD_DIM]
                qu = jnp.concatenate(
                    [qT_ref[h * HEAD_DIM:(h + 1) * HEAD_DIM, :] for h in unit_heads(u)], axis=1)
                return jnp.dot(kc, qu, preferred_element_type=F32)

            for u in range(min(2, n_units)):
                st_ref[u % 2] = logits(u)
            for u in range(n_units):
                g = (u * HEADS_PER_UNIT) // GROUP
                x = st_ref[u % 2] + bias_u
                if exact:
                    m_old = m_ref[u]
                    m_new = jnp.maximum(m_old, jnp.max(x, axis=0, keepdims=True))
                    alpha = jnp.exp2(m_old - m_new)
                    m_ref[u] = m_new
                else:
                    m_new = m_ref[u]
                p = jnp.exp2(x - m_new)
                psum = jnp.sum(p, axis=0, keepdims=True)
                pv = jnp.dot(vT_ref[g * HEAD_DIM:(g + 1) * HEAD_DIM, pl.ds(off, tq)],
                             p.astype(BF16), preferred_element_type=F32)
                if exact:
                    l_ref[u] = alpha * l_ref[u] + psum
                    acc_ref[u] = acc_ref[u] * alpha + pv
                else:
                    l_ref[u] = l_ref[u] + psum
                    acc_ref[u] = acc_ref[u] + pv
                if u + 2 < n_units:
                    st_ref[u % 2] = logits(u + 2)
            return carry

        lax.fori_loop(0, nck, attn_chunk, 0)

    attention_pass(exact=False)
    l_all = l_ref[...]
    in_range = jnp.logical_and(jnp.min(l_all) >= SUM_FLOOR, jnp.max(l_all) < jnp.inf)

    @pl.when(jnp.logical_not(in_range))
    def _():
        attention_pass(exact=True)

    for u in range(n_units):
        ou = acc_ref[u] * (1.0 / l_ref[u])
        for j, h in enumerate(unit_heads(u)):
            o_ref[:, h * HEAD_DIM:(h + 1) * HEAD_DIM] = (
                ou[:, j * tq:(j + 1) * tq].T.astype(o_ref.dtype))


def _prompt_attention(qT, qiT, wT, ki, kb, vT, tq):
    seq = kb.shape[0]
    topk = min(TOPK_MAX, seq // 4)
    const = lambda shape: pl.BlockSpec(shape, lambda i: (0, 0), pipeline_mode=pl.Buffered(1))
    return pl.pallas_call(
        functools.partial(_prompt_attn_kernel, tq=tq, topk=topk, seq=seq),
        grid=(seq // tq,),
        in_specs=[pl.BlockSpec((N_HEADS * HEAD_DIM, tq), lambda i: (0, i)),
                  pl.BlockSpec((IDX_HEADS * IDX_DIM, tq), lambda i: (0, i)),
                  pl.BlockSpec((IDX_HEADS, tq), lambda i: (0, i)),
                  const((seq, IDX_DIM)),
                  const((seq, N_KV_HEADS * HEAD_DIM)),
                  const((N_KV_HEADS * HEAD_DIM, seq))],
        out_specs=pl.BlockSpec((tq, N_HEADS * HEAD_DIM), lambda i: (i, 0)),
        out_shape=jax.ShapeDtypeStruct((seq, N_HEADS * HEAD_DIM), BF16),
        scratch_shapes=[pltpu.VMEM((seq, tq), F32),
                        pltpu.VMEM((N_HEADS // HEADS_PER_UNIT, HEAD_DIM, HEADS_PER_UNIT * tq), F32),
                        pltpu.VMEM((N_HEADS // HEADS_PER_UNIT, 1, HEADS_PER_UNIT * tq), F32),
                        pltpu.VMEM((N_HEADS // HEADS_PER_UNIT, 1, HEADS_PER_UNIT * tq), F32),
                        pltpu.VMEM((2, tq, HEADS_PER_UNIT * tq), F32),
                        pltpu.SMEM((N_KV_HEADS,), F32)],
        compiler_params=_cparams(1),
        name="prompt_attention",
    )(qT, qiT, wT, ki, kb, vT)


def _sample_score_kernel(pt_ref, qi_ref, w_ref, kin_ref, *rest, n_pages, page, t_new):
    ki_refs = rest[:n_pages]
    sc_ref = rest[n_pages]
    del pt_ref
    nt = (((1,), (1,)), ((), ()))
    lane = lax.broadcasted_iota(jnp.int32, (1, page), 1)
    trow = lax.broadcasted_iota(jnp.int32, (t_new, 1), 0)
    qi = qi_ref[...]
    wb = jnp.broadcast_to(w_ref[...] * IDX_SCALE, (IDX_HEADS * t_new, page))

    def head_sum(d):
        r = jnp.maximum(d, 0.0) * wb
        return jnp.sum(r.reshape(IDX_HEADS, t_new, page), axis=0)

    for p in range(n_pages):
        sc_ref[:, p * page:(p + 1) * page] = head_sum(
            jnp.dot(qi, ki_refs[p][...].astype(BF16), preferred_element_type=F32))
    kin = jnp.concatenate(
        [kin_ref[...], jnp.zeros((page - t_new, IDX_DIM), F32)], axis=0).astype(BF16)
    s_new = head_sum(lax.dot_general(qi, kin, nt, preferred_element_type=F32))
    sc_ref[:, n_pages * page:] = jnp.where(lane <= trow, s_new, -jnp.inf)


def _select_kernel(sc_ref, thr_ref, jlim_ref, *, t_new, past, topk):
    rows, total = sc_ref.shape
    kf = float(topk)
    lane = lax.broadcasted_iota(jnp.int32, (1, total), 1)
    trow = lax.broadcasted_iota(jnp.int32, (rows, 1), 0) % t_new

    def count_where(pred):
        return jnp.sum(jnp.where(pred(sc_ref[...], lane), 1.0, 0.0), axis=1, keepdims=True)

    sc_all = sc_ref[...]
    mx = jnp.max(sc_all, axis=1, keepdims=True)
    mn = jnp.min(jnp.where(sc_all == -jnp.inf, jnp.inf, sc_all), axis=1, keepdims=True)
    need = (past + trow + 1) > topk
    thr, tie = _bisect_threshold(
        lambda t: count_where(lambda s, kp: s >= t), mn, mx, need, kf)
    thr = jnp.where(need, thr, -jnp.inf)

    any_tie = jnp.max(tie) > 0.5
    nbits = max(1, (total - 1).bit_length())

    def tie_search(_):
        c_gt = count_where(lambda s, kp: s > thr)
        r = kf - c_gt

        def body(b, j):
            cand = j + jnp.left_shift(jnp.int32(1), nbits - 1 - b)
            cnt = count_where(lambda s, kp: jnp.logical_and(s == thr, kp < cand))
            return jnp.where(cnt < r, cand, j)
        j = lax.fori_loop(0, nbits, body, jnp.zeros((rows, 1), jnp.int32))
        return jnp.where(tie > 0.5, j, jnp.int32(total))

    jlim = lax.cond(any_tie, tie_search,
                    lambda _: jnp.full((rows, 1), total, jnp.int32), 0)
    thr_ref[...] = thr
    jlim_ref[...] = jnp.where(need, jlim, -1)


def _sample_attn_kernel(pt_ref, sc_ref, thr_ref, jlim_ref, q_ref, kn_ref, vn_ref, *rest,
                        n_pages, page, t_new):
    k_refs = rest[:n_pages]
    v_refs = rest[n_pages:2 * n_pages]
    o_ref = rest[2 * n_pages]
    s_ref = rest[2 * n_pages + 1]
    del pt_ref
    nchunk = n_pages + 1
    nt = (((1,), (1,)), ((), ()))

    def pad_rows(x, rows):
        return jnp.concatenate(
            [x, jnp.zeros((rows - x.shape[0], x.shape[1]), x.dtype)], axis=0).astype(BF16)

    lane = lax.broadcasted_iota(jnp.int32, (1, page), 1)
    thr = thr_ref[...]
    jlim = jlim_ref[...]

    q = q_ref[...]
    rows_g = GROUP * t_new

    def head_rows(ref, g, n_keys):
        return ref[pl.ds(g, n_keys, stride=N_KV_HEADS), :]

    def logits_chunk(c, k_of):
        s = sc_ref[:, c * page:(c + 1) * page]
        kpos = c * page + lane
        sel = jnp.logical_or(s > thr, jnp.logical_and(s == thr, kpos <= jlim))
        bias = jnp.where(sel, 0.0, MASKED)
        bias4 = jnp.concatenate([bias] * GROUP, axis=0)
        for g in range(N_KV_HEADS):
            st = lax.dot_general(q[g * rows_g:(g + 1) * rows_g, :], k_of(g), nt,
                                 preferred_element_type=F32)
            s_ref[g * rows_g:(g + 1) * rows_g, c * page:(c + 1) * page] = st + bias4

    for p in range(n_pages):
        logits_chunk(p, lambda g, p=p: head_rows(k_refs[p], g, page).astype(BF16))
    logits_chunk(n_pages, lambda g: pad_rows(head_rows(kn_ref, g, t_new), page))

    s_all = s_ref[...]
    m = jnp.max(s_all, axis=1, keepdims=True)
    pr = jnp.exp2(s_all - m)
    l = jnp.sum(pr, axis=1, keepdims=True)
    s_ref[...] = pr

    accs = [jnp.zeros((rows_g, HEAD_DIM), F32) for _ in range(N_KV_HEADS)]
    for c in range(nchunk):
        for g in range(N_KV_HEADS):
            vc = (head_rows(v_refs[c], g, page).astype(BF16) if c < n_pages
                  else pad_rows(head_rows(vn_ref, g, t_new), page))
            pc = s_ref[g * rows_g:(g + 1) * rows_g, c * page:(c + 1) * page].astype(BF16)
            accs[g] = accs[g] + jnp.dot(pc, vc, preferred_element_type=F32)

    inv = 1.0 / l
    for g in range(N_KV_HEADS):
        og = accs[g] * inv[g * rows_g:(g + 1) * rows_g, :]
        for j in range(GROUP):
            h = g * GROUP + j
            o_ref[:, h * HEAD_DIM:(h + 1) * HEAD_DIM] = (
                og[j * t_new:(j + 1) * t_new, :].astype(o_ref.dtype))


def _sample_attention(layer, page_table, qi_s, w_s, q_s, ki_new, k_new, v_new,
                      cache_ki, cache_k, cache_v):
    db, n_pages = page_table.shape
    page = cache_k.shape[2]
    t_new = ki_new.shape[1]
    past = n_pages * page
    topk = min(TOPK_MAX, (past + t_new) // 4)
    rows = N_HEADS * t_new
    kv_rows = page * N_KV_HEADS
    new_rows = t_new * N_KV_HEADS
    total = (n_pages + 1) * page

    def per_seq(shape):
        return pl.BlockSpec((None,) + shape, lambda b, pt: (b, 0, 0))

    def paged(shape, p):
        return pl.BlockSpec((None, None) + shape,
                            lambda b, pt, p=p: (layer, pt[b, p], 0, 0))

    ck = cache_k.reshape(cache_k.shape[0], cache_k.shape[1], kv_rows, HEAD_DIM)
    cv = cache_v.reshape(cache_v.shape[0], cache_v.shape[1], kv_rows, HEAD_DIM)
    ckit = jnp.swapaxes(cache_ki, 2, 3)

    scores = pl.pallas_call(
        functools.partial(_sample_score_kernel, n_pages=n_pages, page=page, t_new=t_new),
        grid_spec=pltpu.PrefetchScalarGridSpec(
            num_scalar_prefetch=1,
            grid=(db,),
            in_specs=[per_seq((rows, IDX_DIM)), per_seq((rows, 1)), per_seq((t_new, IDX_DIM))]
            + [paged((IDX_DIM, page), p) for p in range(n_pages)],
            out_specs=per_seq((t_new, total)),
        ),
        out_shape=jax.ShapeDtypeStruct((db, t_new, total), F32),
        compiler_params=_cparams(1),
        name="sample_scores",
    )(page_table, qi_s, w_s, ki_new, *([ckit] * n_pages))

    thr, jlim = pl.pallas_call(
        functools.partial(_select_kernel, t_new=t_new, past=past, topk=topk),
        out_shape=[jax.ShapeDtypeStruct((db * t_new, 1), F32),
                   jax.ShapeDtypeStruct((db * t_new, 1), jnp.int32)],
        compiler_params=pltpu.CompilerParams(vmem_limit_bytes=VMEM_LIMIT),
        name="sample_select",
    )(scores.reshape(db * t_new, total))

    return pl.pallas_call(
        functools.partial(_sample_attn_kernel, n_pages=n_pages, page=page, t_new=t_new),
        grid_spec=pltpu.PrefetchScalarGridSpec(
            num_scalar_prefetch=1,
            grid=(db,),
            in_specs=[per_seq((t_new, total)), per_seq((t_new, 1)), per_seq((t_new, 1)),
                      per_seq((rows, HEAD_DIM)), per_seq((new_rows, HEAD_DIM)),
                      per_seq((new_rows, HEAD_DIM))]
            + [paged((kv_rows, HEAD_DIM), p) for p in range(n_pages)]
            + [paged((kv_rows, HEAD_DIM), p) for p in range(n_pages)],
            out_specs=pl.BlockSpec((None, t_new, N_HEADS * HEAD_DIM), lambda b, pt: (b, 0, 0)),
            scratch_shapes=[pltpu.VMEM((rows, total), F32)],
        ),
        out_shape=jax.ShapeDtypeStruct((db, t_new, N_HEADS * HEAD_DIM), F32),
        compiler_params=_cparams(1),
        name="sample_attention",
    )(page_table, scores, thr.reshape(db, t_new, 1), jlim.reshape(db, t_new, 1), q_s,
      k_new, v_new, *([ck] * n_pages), *([cv] * n_pages))


def _conv_kernel(u_ref, halo_ref, cb_ref, w_ref, o_ref, *, seq_len, tm):
    i = pl.program_id(0)
    u = u_ref[...]
    halo = halo_ref[...]
    r = lax.broadcasted_iota(jnp.int32, (tm, 1), 0)
    if seq_len >= tm:
        t = r
        halo = jnp.where(i > 0, halo, 0.0)
        h7 = halo[SUBLANES - 1:SUBLANES, :]
        h6 = halo[SUBLANES - 2:SUBLANES - 1, :]
        pre1 = jnp.broadcast_to(h7, u.shape)
        pre2 = jnp.where(r == 0, h6, h7)
    else:
        t = r % seq_len
        pre2 = halo
        pre1 = pltpu.roll(halo, tm - 1, 0)
    u1 = jnp.where(t >= 1, pltpu.roll(u, 1, 0), pre1)
    u2 = jnp.where(t >= 2, pltpu.roll(u, 2, 0), pre2)
    y = w_ref[0:1, :] * u2 + w_ref[1:2, :] * u1 + w_ref[2:3, :] * u
    o_ref[...] = (cb_ref[...] * y).astype(o_ref.dtype)


def _conv_branch(u, cb, w_conv, history, seq_len):
    t, c = u.shape
    tm = _row_tile(t)
    if history is None:
        halo_arr = u
        halo_spec = pl.BlockSpec(
            (SUBLANES, c), lambda i: (jnp.maximum(i * (tm // SUBLANES) - 1, 0), 0))
    else:
        halo_arr = history
        halo_spec = pl.BlockSpec((tm, c), lambda i: (i, 0))
    return pl.pallas_call(
        functools.partial(_conv_kernel, seq_len=seq_len, tm=tm),
        grid=(t // tm,),
        in_specs=[pl.BlockSpec((tm, c), lambda i: (i, 0)), halo_spec,
                  pl.BlockSpec((tm, c), lambda i: (i, 0)),
                  pl.BlockSpec((CONV_WIDTH, c), lambda i: (0, 0))],
        out_specs=pl.BlockSpec((tm, c), lambda i: (i, 0)),
        out_shape=jax.ShapeDtypeStruct((t, c), BF16),
        compiler_params=_cparams(1),
        name="short_conv",
    )(u, halo_arr, cb, w_conv)


def _merge_kernel(a_ref, c_ref, wa_ref, wc_ref, ga_ref, gb_ref, o_ref):
    a = jnp.dot(a_ref[...], wa_ref[...].astype(BF16), preferred_element_type=F32)
    c = jnp.dot(c_ref[...], wc_ref[...].astype(BF16), preferred_element_type=F32)
    o_ref[...] = (ga_ref[...] * a + gb_ref[...] * c).astype(o_ref.dtype)


def _merge(a, cpre, w_ao, w_co, layer, gates, tn=512):
    t, d = a.shape[0], w_ao.shape[2]
    tm = _row_tile(t, BIG_ROW_TILE)
    nj = d // tn
    return pl.pallas_call(
        _merge_kernel,
        grid=(t // tm, nj),
        in_specs=[pl.BlockSpec((tm, a.shape[1]), lambda i, j: (i, 0)),
                  pl.BlockSpec((tm, cpre.shape[1]), lambda i, j: (i, 0)),
                  _layer_block(layer, (w_ao.shape[1], tn), lambda i, j: (0, j)),
                  _layer_block(layer, (w_co.shape[1], tn), lambda i, j: (0, j)),
                  pl.BlockSpec((tm, tn), lambda i, j: (i, j)),
                  pl.BlockSpec((tm, tn), lambda i, j: (i, j + nj))],
        out_specs=pl.BlockSpec((tm, tn), lambda i, j: (i, j)),
        out_shape=jax.ShapeDtypeStruct((t, d), BF16),
        compiler_params=_cparams(2),
        name="gated_merge",
    )(a, cpre, w_ao, w_co, gates, gates)


def _out_proj_kernel(m_ref, w_ref, x_ref, g_ref, x1_ref, xn_ref):
    h = jnp.dot(m_ref[...], w_ref[...], preferred_element_type=F32)
    x1 = x_ref[...] + h
    x1_ref[...] = x1
    xn_ref[...] = _rms_body(x1, g_ref[...]).astype(xn_ref.dtype)


def _out_proj(merged, w_o, layer, x, g):
    t, d = x.shape
    tm = _row_tile(t)
    return pl.pallas_call(
        _out_proj_kernel,
        grid=(t // tm,),
        in_specs=[pl.BlockSpec((tm, d), lambda i: (i, 0)),
                  _layer_block(layer, (d, d), lambda i: (0, 0), pipeline_mode=pl.Buffered(1)),
                  pl.BlockSpec((tm, d), lambda i: (i, 0)),
                  pl.BlockSpec((1, d), lambda i: (0, 0))],
        out_specs=[pl.BlockSpec((tm, d), lambda i: (i, 0)),
                   pl.BlockSpec((tm, d), lambda i: (i, 0))],
        out_shape=[jax.ShapeDtypeStruct((t, d), F32), jax.ShapeDtypeStruct((t, d), BF16)],
        compiler_params=_cparams(1),
        name="out_proj",
    )(merged, w_o, x, g.reshape(1, d))


def _ffn_in_kernel(x_ref, wg_ref, wu_ref, o_ref):
    x = x_ref[...]
    g = jnp.dot(x, wg_ref[...].astype(BF16), preferred_element_type=F32)
    u = jnp.dot(x, wu_ref[...].astype(BF16), preferred_element_type=F32)
    o_ref[...] = (jax.nn.silu(g) * u).astype(o_ref.dtype)


def _ffn_in(xn, w_fi, layer, tn=512):
    t, d = xn.shape
    d_ff = w_fi.shape[2] // 2
    tm = _row_tile(t, BIG_ROW_TILE)
    nj = d_ff // tn
    return pl.pallas_call(
        _ffn_in_kernel,
        grid=(t // tm, nj),
        in_specs=[pl.BlockSpec((tm, d), lambda i, j: (i, 0)),
                  _layer_block(layer, (d, tn), lambda i, j: (0, j)),
                  _layer_block(layer, (d, tn), lambda i, j: (0, j + nj))],
        out_specs=pl.BlockSpec((tm, tn), lambda i, j: (i, j)),
        out_shape=jax.ShapeDtypeStruct((t, d_ff), BF16),
        compiler_params=_cparams(2),
        name="ffn_in",
    )(xn, w_fi, w_fi)


def _ffn_out_kernel(a_ref, w_ref, x_ref, o_ref):
    o_ref[...] = x_ref[...] + jnp.dot(a_ref[...], w_ref[...], preferred_element_type=F32)


def _ffn_out(act, w_fo, layer, x1, tn=512):
    t, d = x1.shape
    d_ff = act.shape[1]
    tm = _row_tile(t, BIG_ROW_TILE)
    return pl.pallas_call(
        _ffn_out_kernel,
        grid=(t // tm, d // tn),
        in_specs=[pl.BlockSpec((tm, d_ff), lambda i, j: (i, 0)),
                  _layer_block(layer, (d_ff, tn), lambda i, j: (0, j)),
                  pl.BlockSpec((tm, tn), lambda i, j: (i, j))],
        out_specs=pl.BlockSpec((tm, tn), lambda i, j: (i, j)),
        out_shape=jax.ShapeDtypeStruct((t, d), F32),
        compiler_params=_cparams(2),
        name="ffn_out",
    )(act, w_fo, x1)


def _rope_tables(pos, dim, reps):
    inv = ROPE_THETA ** (-jnp.arange(0, dim, 2, dtype=F32) / dim)
    ang = pos.astype(F32)[:, None] * inv[None, :]
    cos, sin = jnp.cos(ang), jnp.sin(ang)
    return (jnp.tile(jnp.concatenate([cos, cos], axis=1), (1, reps)),
            jnp.tile(jnp.concatenate([-sin, sin], axis=1), (1, reps)))


def _split_w_in(w, d_model):
    attn, kv, conv = N_HEADS * HEAD_DIM, N_KV_HEADS * HEAD_DIM, d_model // 2
    sizes = (attn, kv, kv, IDX_HEADS * IDX_DIM, IDX_DIM, IDX_HEADS, conv, conv, conv,
             d_model, d_model)
    parts, off = [], 0
    for n in sizes:
        parts.append(w[:, off:off + n].astype(BF16))
        off += n
    wq, wk, wv, wqi, wki, wwi, wcb, wcc, wch, wga, wgb = parts
    pad = jnp.zeros((w.shape[0], LANES - IDX_DIM - IDX_HEADS), BF16)
    return dict(q=wq, k=wk, v=wv, qi=wqi, kiwi=jnp.concatenate([wki, wwi, pad], axis=1),
                cb=wcb, cc=wcc, ch=wch, gates=jnp.concatenate([wga, wgb], axis=1))


def _layer(x, wl, tabs, attn_fn, history, seq_len, attn_cols):
    cos128, sin128, cos64, sin64, cos_kw, sin_kw = tabs
    xn = _rmsnorm(x, wl["norm_mix"], BF16)
    rope128 = functools.partial(_proj_rope_kernel, rope=_rope128)
    rope64 = functools.partial(_proj_rope_kernel, rope=_rope64)
    qlay = "cols" if attn_cols else "rows"
    (q,) = _proj(functools.partial(rope128, scale=Q_SCALE), xn, [wl["q"]],
                 [cos128, sin128], [BF16], 512, "proj_q", (qlay,))
    k, kb = _proj(functools.partial(rope128, scale=1.0), xn, [wl["k"]],
                  [cos128, sin128], [F32, BF16], 512, "proj_k", ("heads", "rows"))
    v, vb = _proj(_proj_plain_kernel, xn, [wl["v"]], [], [F32, BF16], 512, "proj_v",
                  ("heads", qlay))
    (qi,) = _proj(functools.partial(rope64, scale=1.0), xn, [wl["qi"]],
                  [cos64, sin64], [BF16], 512, "proj_qi", (qlay,))
    (kiwi,) = _proj(functools.partial(rope64, scale=1.0), xn, [wl["kiwi"]],
                    [cos_kw, sin_kw], [F32], LANES, "proj_kiwi", ("rows",))
    cb, u = _proj(_proj_conv_kernel, xn, [wl["cb"], wl["cc"], wl["ch"]], [],
                  [F32, F32], 512, "proj_conv")
    (gates,) = _proj(_proj_gate_kernel, xn, [wl["gates"]], [], [BF16], 512, "proj_gates")
    ki = kiwi[:, :IDX_DIM]
    wi = kiwi[:, IDX_DIM:IDX_DIM + IDX_HEADS]

    a = attn_fn(q=q, k=k, kb=kb, v=v, vb=vb, qi=qi, ki=ki, wi=wi)
    cpre = _conv_branch(u, cb, wl["w_conv"], history, seq_len)
    l = wl["layer"]
    merged = _merge(a, cpre, wl["w_attn_out"], wl["w_conv_out"], l, gates)
    x1, xn2 = _out_proj(merged, wl["w_o"], l, x, wl["norm_ffn"])
    act = _ffn_in(xn2, wl["w_ffn_in"], l)
    x2 = _ffn_out(act, wl["w_ffn_out"], l, x1)
    return x2, k, v, ki, u


def kernel(x_prompt, x_sample, cache_k, cache_v, cache_ki, state_conv, page_table,
           norm_mix, w_in, w_conv, w_attn_out, w_conv_out, w_o, norm_ffn, w_ffn_in,
           w_ffn_out, norm_final):
    depth = w_in.shape[0]
    bp, seq, d_model = x_prompt.shape
    db, t_new, _ = x_sample.shape
    page = cache_k.shape[2]
    past = page_table.shape[1] * page
    conv_dim = d_model // 2

    pos_p = jnp.arange(seq)
    pos_s = jnp.tile(past + jnp.arange(t_new), db)

    def tables(pos):
        c128, s128 = _rope_tables(pos, HEAD_DIM, 1)
        c64, s64 = _rope_tables(pos, IDX_DIM, LANES // IDX_DIM)
        keep = jnp.arange(LANES)[None, :] < IDX_DIM
        return (c128, s128, c64, s64, jnp.where(keep, c64, 1.0), jnp.where(keep, s64, 0.0))

    tabs_p, tabs_s = tables(pos_p), tables(pos_s)

    stacked = dict(w_attn_out=w_attn_out, w_conv_out=w_conv_out,
                   w_o=w_o.astype(BF16), w_ffn_in=w_ffn_in,
                   w_ffn_out=w_ffn_out.astype(BF16))
    layers = []
    for l in range(depth):
        wl = _split_w_in(w_in[l], d_model)
        wl.update(layer=l, norm_mix=norm_mix[l], norm_ffn=norm_ffn[l], w_conv=w_conv[l],
                  **stacked)
        layers.append(wl)

    tq = min(256, seq)

    def prompt_attn(q, k, kb, v, vb, qi, ki, wi):
        del k, v
        return _prompt_attention(q, qi, wi.T, ki.astype(BF16), kb, vb, tq)

    def head_major(a, width):
        nh = a.shape[1] // width
        return a.reshape(db, t_new, nh, width).transpose(0, 2, 1, 3).reshape(
            db, nh * t_new, width)

    def sample_attn_for(l):
        def fn(q, k, kb, v, vb, qi, ki, wi):
            del kb, vb
            a = _sample_attention(
                l, page_table, head_major(qi, IDX_DIM), head_major(wi, 1),
                head_major(q, HEAD_DIM), ki.reshape(db, t_new, IDX_DIM),
                k.reshape(db, t_new * N_KV_HEADS, HEAD_DIM),
                v.reshape(db, t_new * N_KV_HEADS, HEAD_DIM),
                cache_ki, cache_k, cache_v)
            return a.reshape(db * t_new, N_HEADS * HEAD_DIM).astype(BF16)
        return fn

    def run(x, tabs, make_attn, history_for, seq_len, attn_cols):
        ks, vs, kis, us = [], [], [], []
        for l in range(depth):
            x, k, v, ki, u = _layer(x, layers[l], tabs, make_attn(l),
                                    history_for(l), seq_len, attn_cols)
            ks.append(k)
            vs.append(v)
            kis.append(ki)
            us.append(u)
        return _rmsnorm(x, norm_final, F32), ks, vs, kis, us

    assert bp == 1
    y_p, ks, vs, kis, us = run(x_prompt[0], tabs_p, lambda l: prompt_attn,
                               lambda l: None, seq, True)
    y_prompt = y_p[None]
    new_k_p = jnp.stack(ks).reshape(depth, bp, seq, N_KV_HEADS, HEAD_DIM)
    new_v_p = jnp.stack(vs).reshape(depth, bp, seq, N_KV_HEADS, HEAD_DIM)
    new_ki_p = jnp.stack(kis).reshape(depth, bp, seq, IDX_DIM)
    new_conv_p = jnp.stack([u[seq - (CONV_WIDTH - 1):] for u in us]).reshape(
        depth, bp, CONV_WIDTH - 1, conv_dim)

    def history_for(l):
        h = state_conv[l]
        pad = jnp.zeros((db, t_new - (CONV_WIDTH - 1), conv_dim), F32)
        return jnp.concatenate([h, pad], axis=1).reshape(db * t_new, conv_dim)

    y_s, ks, vs, kis, us = run(x_sample.reshape(db * t_new, d_model), tabs_s,
                               sample_attn_for, history_for, t_new, False)
    y_sample = y_s.reshape(db, t_new, d_model)
    new_k_s = jnp.stack(ks).reshape(depth, db, t_new, N_KV_HEADS, HEAD_DIM)
    new_v_s = jnp.stack(vs).reshape(depth, db, t_new, N_KV_HEADS, HEAD_DIM)
    new_ki_s = jnp.stack(kis).reshape(depth, db, t_new, IDX_DIM)
    new_conv_s = jnp.stack(
        [u.reshape(db, t_new, conv_dim)[:, t_new - (CONV_WIDTH - 1):] for u in us])

    return (y_prompt, y_sample, new_k_p, new_v_p, new_ki_p, new_conv_p,
            new_k_s, new_v_s, new_ki_s, new_conv_s)
```

```python
import functools

import jax
import jax.numpy as jnp
from jax import lax
from jax.experimental import pallas as pl
from jax.experimental.pallas import tpu as pltpu

N_HEADS = 16
N_KV_HEADS = 4
HEAD_DIM = 128
GROUP = N_HEADS // N_KV_HEADS
IDX_HEADS = 16
IDX_DIM = 64
TOPK_MAX = 256
CONV_WIDTH = 3
ROPE_THETA = 10000.0
NORM_EPS = 1e-6
IDX_SCALE = IDX_HEADS ** -0.5 * IDX_DIM ** -0.5
LOG2_E = 1.4426950408889634
Q_SCALE = HEAD_DIM ** -0.5 * LOG2_E

LANES = 128
SUBLANES = 8
VMEM_LIMIT = 48 * 1024 * 1024
MASKED = -1e30
MAX_BISECT = 200
HEADS_PER_UNIT = 4
SUM_FLOOR = 2.0 ** -100
SCORE_SEQS = 2
COUNT_ROWS = 32
BIG_ROW_TILE = 1024

F32 = jnp.float32
BF16 = jnp.bfloat16


def _cparams(n_grid):
    return pltpu.CompilerParams(
        dimension_semantics=("arbitrary",) * n_grid, vmem_limit_bytes=VMEM_LIMIT)


def _row_tile(t, cap=512):
    return min(cap, t)


def _layer_block(layer, block, index_map, **kw):
    return pl.BlockSpec((None,) + block, lambda *a: (layer,) + index_map(*a), **kw)


def _rms_body(x, g):
    y = x * lax.rsqrt(jnp.mean(x * x, axis=-1, keepdims=True) + NORM_EPS)
    return y * g


def _rmsnorm_kernel(x_ref, g_ref, o_ref):
    o_ref[...] = _rms_body(x_ref[...], g_ref[...]).astype(o_ref.dtype)


def _rmsnorm(x, g, out_dtype):
    t, d = x.shape
    tm = _row_tile(t)
    return pl.pallas_call(
        _rmsnorm_kernel,
        grid=(t // tm,),
        in_specs=[pl.BlockSpec((tm, d), lambda i: (i, 0)),
                  pl.BlockSpec((1, d), lambda i: (0, 0))],
        out_specs=pl.BlockSpec((tm, d), lambda i: (i, 0)),
        out_shape=jax.ShapeDtypeStruct((t, d), out_dtype),
        compiler_params=_cparams(1),
        name="rmsnorm",
    )(x, g.reshape(1, d))


def _rope128(z, cos, sin):
    outs = []
    for c in range(z.shape[1] // LANES):
        zc = z[:, c * LANES:(c + 1) * LANES]
        outs.append(zc * cos + pltpu.roll(zc, HEAD_DIM // 2, 1) * sin)
    return outs[0] if len(outs) == 1 else jnp.concatenate(outs, axis=1)


def _rope64(z, cos, sin):
    lane = lax.broadcasted_iota(jnp.int32, (1, LANES), 1)
    first_half = (lane % IDX_DIM) < (IDX_DIM // 2)
    outs = []
    for c in range(z.shape[1] // LANES):
        zc = z[:, c * LANES:(c + 1) * LANES]
        rot = jnp.where(first_half,
                        pltpu.roll(zc, LANES - IDX_DIM // 2, 1),
                        pltpu.roll(zc, IDX_DIM // 2, 1))
        outs.append(zc * cos + rot * sin)
    return outs[0] if len(outs) == 1 else jnp.concatenate(outs, axis=1)


def _rope_cols(zt, cos_t, sin_t, width):
    half = width // 2
    outs = []
    for c in range(zt.shape[0] // LANES):
        blk = zt[c * LANES:(c + 1) * LANES, :]
        swapped = jnp.concatenate(
            [blk[b + half:b + width, :] if first else blk[b:b + half, :]
             for b in range(0, LANES, width) for first in (True, False)], axis=0)
        outs.append(blk * cos_t + swapped * sin_t)
    return outs[0] if len(outs) == 1 else jnp.concatenate(outs, axis=0)


def _store_all(z, o_refs, layouts):
    zt = z.T if "cols" in layouts else None
    for o, layout in zip(o_refs, layouts):
        if layout == "cols":
            o[...] = zt.astype(o.dtype)
        elif layout == "heads":
            n_heads = z.shape[1] // HEAD_DIM
            for g in range(n_heads):
                o[pl.ds(g, z.shape[0], stride=n_heads), :] = (
                    z[:, g * HEAD_DIM:(g + 1) * HEAD_DIM].astype(o.dtype))
        else:
            o[...] = z.astype(o.dtype)


def _proj_rope_kernel(x_ref, w_ref, cos_ref, sin_ref, *o_refs, rope, scale, layouts):
    z = jnp.dot(x_ref[...], w_ref[...].astype(BF16), preferred_element_type=F32)
    if scale != 1.0:
        z = z * scale
    if layouts == ("cols",):
        width = HEAD_DIM if rope is _rope128 else IDX_DIM
        zt = _rope_cols(z.T, cos_ref[...], sin_ref[...], width)
        o_refs[0][...] = zt.astype(o_refs[0].dtype)
    else:
        _store_all(rope(z, cos_ref[...], sin_ref[...]), o_refs, layouts)


def _proj_plain_kernel(x_ref, w_ref, *o_refs, layouts):
    z = jnp.dot(x_ref[...], w_ref[...].astype(BF16), preferred_element_type=F32)
    _store_all(z, o_refs, layouts)


def _proj_gate_kernel(x_ref, w_ref, o_ref):
    z = jnp.dot(x_ref[...], w_ref[...], preferred_element_type=F32)
    o_ref[...] = jax.nn.sigmoid(z).astype(o_ref.dtype)


def _proj_conv_kernel(x_ref, wb_ref, wc_ref, wh_ref, cb_ref, u_ref):
    x = x_ref[...]
    cb_ref[...] = jnp.dot(x, wb_ref[...], preferred_element_type=F32)
    cc = jnp.dot(x, wc_ref[...], preferred_element_type=F32)
    ch = jnp.dot(x, wh_ref[...], preferred_element_type=F32)
    u_ref[...] = cc * ch


def _proj(kernel, xn, ws, tables, out_dtypes, tn, name, layouts=None):
    t, k = xn.shape
    n = ws[0][3] if isinstance(ws[0], tuple) else ws[0].shape[1]
    tm = _row_tile(t, BIG_ROW_TILE)
    tn = min(tn, n)

    def w_spec(w):
        if isinstance(w, tuple):
            _, layer, col0, _ = w
            return _layer_block(layer, (k, tn), lambda i, j: (0, col0 // tn + j))
        return pl.BlockSpec((k, tn), lambda i, j: (0, j))
    if layouts is None:
        layouts = ("rows",) * len(out_dtypes)
    else:
        kernel = functools.partial(kernel, layouts=layouts)
    n_heads = n // HEAD_DIM

    def out_spec(layout):
        if layout == "cols":
            return pl.BlockSpec((tn, tm), lambda i, j: (j, i))
        if layout == "heads":
            assert tn == n
            return pl.BlockSpec((tm * n_heads, HEAD_DIM), lambda i, j: (i, 0))
        return pl.BlockSpec((tm, tn), lambda i, j: (i, j))

    def out_dims(layout):
        return {"cols": (n, t), "heads": (t * n_heads, HEAD_DIM)}.get(layout, (t, n))

    in_specs = [pl.BlockSpec((tm, k), lambda i, j: (i, 0))]
    in_specs += [w_spec(w) for w in ws]
    ws = [w[0] if isinstance(w, tuple) else w for w in ws]
    if tables and layouts == ("cols",):
        tables = [tb.T for tb in tables]
        in_specs += [pl.BlockSpec((LANES, tm), lambda i, j: (0, i)) for _ in tables]
    else:
        in_specs += [pl.BlockSpec((tm, LANES), lambda i, j: (i, 0)) for _ in tables]
    out_specs = [out_spec(layout) for layout in layouts]
    out_shape = [jax.ShapeDtypeStruct(out_dims(layout), dt)
                 for dt, layout in zip(out_dtypes, layouts)]
    outs = pl.pallas_call(
        kernel,
        grid=(t // tm, n // tn),
        in_specs=in_specs,
        out_specs=out_specs,
        out_shape=out_shape,
        compiler_params=_cparams(2),
        name=name,
    )(xn, *ws, *tables)
    return outs


def _bisect_threshold(count_ge, lo0, hi0, need, kf, steps_per_check=1):
    big = jnp.maximum(jnp.abs(hi0) * 1e-6, 1e-30)
    hi0 = hi0 + big
    done0 = jnp.where(need, 0.0, 1.0)
    zeros = jnp.zeros_like(lo0)

    def step(lo, hi, thr, done, tie):
        mid = 0.5 * lo + 0.5 * hi
        cnt = count_ge(mid)
        live = done < 0.5
        collapsed = jnp.logical_or(mid <= lo, mid >= hi)
        found = jnp.logical_and(cnt == kf, jnp.logical_not(collapsed))
        fin = jnp.logical_and(live, jnp.logical_or(found, collapsed))
        thr = jnp.where(jnp.logical_and(live, found), mid, thr)
        thr = jnp.where(jnp.logical_and(live, collapsed), lo, thr)
        tie = jnp.where(jnp.logical_and(live, collapsed), 1.0, tie)
        done = jnp.where(fin, 1.0, done)
        go_up = jnp.logical_and(live, cnt > kf)
        go_dn = jnp.logical_and(live, cnt < kf)
        lo = jnp.where(go_up, mid, lo)
        hi = jnp.where(go_dn, mid, hi)
        return lo, hi, thr, done, tie

    def cond(c):
        it, pending = c[0], c[1]
        return jnp.logical_and(it < MAX_BISECT, pending > 0)

    def body(c):
        it, _, lo, hi, thr, done, tie = c
        for _ in range(steps_per_check):
            lo, hi, thr, done, tie = step(lo, hi, thr, done, tie)
        pending = (jnp.min(done) < 0.5).astype(jnp.int32)
        return it + steps_per_check, pending, lo, hi, thr, done, tie

    _, _, lo, _, thr, done, tie = lax.while_loop(
        cond, body, (jnp.int32(0), jnp.int32(1), lo0, hi0, lo0, done0, zeros))
    thr = jnp.where(done > 0.5, thr, lo)
    return thr, tie


def _prompt_attn_kernel(qT_ref, qiT_ref, wT_ref, ki_ref, k_ref, vT_ref, o_ref,
                        sc_ref, acc_ref, m_ref, l_ref, st_ref, knorm_ref, *, tq, topk, seq):
    i = pl.program_id(0)
    nck = i + 1
    qpos = i * tq + lax.broadcasted_iota(jnp.int32, (1, tq), 1)
    row = lax.broadcasted_iota(jnp.int32, (tq, 1), 0)
    kf = float(topk)

    def score_chunk(c, carry):
        mx, mn = carry
        off = pl.multiple_of(c * tq, tq)
        kic = ki_ref[pl.ds(off, tq), :]
        acc = jnp.zeros((tq, tq), F32)
        for h in range(IDX_HEADS):
            d = jnp.dot(kic, qiT_ref[h * IDX_DIM:(h + 1) * IDX_DIM, :],
                        preferred_element_type=F32)
            acc = acc + jnp.maximum(d, 0.0) * (wT_ref[h:h + 1, :] * IDX_SCALE)
        valid = (off + row) <= qpos
        sc_ref[pl.ds(off, tq), :] = jnp.where(valid, acc, -jnp.inf)
        mx = jnp.maximum(mx, jnp.max(jnp.where(valid, acc, -jnp.inf), axis=0, keepdims=True))
        mn = jnp.minimum(mn, jnp.min(jnp.where(valid, acc, jnp.inf), axis=0, keepdims=True))
        return mx, mn

    mx, mn = lax.fori_loop(
        0, nck, score_chunk,
        (jnp.full((1, tq), -jnp.inf, F32), jnp.full((1, tq), jnp.inf, F32)))

    def count_where(pred):
        def body(c, acc):
            off = pl.multiple_of(c * tq, tq)
            ind = jnp.where(pred(sc_ref[pl.ds(off, tq), :], off + row), 1.0, 0.0)
            return acc + jnp.sum(ind.reshape(tq // COUNT_ROWS, COUNT_ROWS, tq), axis=0)
        acc = lax.fori_loop(0, nck, body, jnp.zeros((COUNT_ROWS, tq), F32))
        return jnp.sum(acc, axis=0, keepdims=True)

    need = (qpos + 1) > topk
    thr, tie = _bisect_threshold(
        lambda t: count_where(lambda s, kp: s >= t), mn, mx, need, kf, steps_per_check=2)
    thr = jnp.where(need, thr, -jnp.inf)

    any_tie = jnp.max(tie) > 0.5
    nbits = max(1, (seq - 1).bit_length())

    def tie_search(_):
        c_gt = count_where(lambda s, kp: s > thr)
        r = kf - c_gt

        def body(b, j):
            cand = j + jnp.left_shift(jnp.int32(1), nbits - 1 - b)
            cnt = count_where(lambda s, kp: jnp.logical_and(s == thr, kp < cand))
            return jnp.where(cnt < r, cand, j)
        j = lax.fori_loop(0, nbits, body, jnp.zeros((1, tq), jnp.int32))
        return jnp.where(tie > 0.5, j, jnp.int32(seq))

    jlim = lax.cond(any_tie, tie_search,
                    lambda _: jnp.full((1, tq), seq, jnp.int32), 0)
    jlim = jnp.where(need, jlim, -1)

    n_units = N_HEADS // HEADS_PER_UNIT

    @pl.when(i == 0)
    def _():
        rows = min(512, seq)
        for g in range(N_KV_HEADS):
            def body(c, mx):
                kc = k_ref[pl.ds(pl.multiple_of(c * rows, rows), rows),
                           g * HEAD_DIM:(g + 1) * HEAD_DIM].astype(F32)
                return jnp.maximum(mx, jnp.max(jnp.sum(kc * kc, axis=1, keepdims=True)))
            knorm_ref[g] = jnp.sqrt(lax.fori_loop(0, seq // rows, body, jnp.float32(0.0)))

    def unit_heads(u):
        return range(u * HEADS_PER_UNIT, (u + 1) * HEADS_PER_UNIT)

    def attention_pass(exact):
        l_ref[...] = jnp.zeros(l_ref.shape, F32)
        acc_ref[...] = jnp.zeros(acc_ref.shape, F32)
        if exact:
            m_ref[...] = jnp.full(m_ref.shape, MASKED, F32)
        else:
            for u in range(n_units):
                norms = []
                for h in unit_heads(u):
                    qh = qT_ref[h * HEAD_DIM:(h + 1) * HEAD_DIM, :].astype(F32)
                    norms.append(jnp.sqrt(jnp.sum(qh * qh, axis=0, keepdims=True))
                                 * knorm_ref[h // GROUP])
                m_ref[u] = jnp.concatenate(norms, axis=1)

        def attn_chunk(c, carry):
            off = pl.multiple_of(c * tq, tq)
            s = sc_ref[pl.ds(off, tq), :]
            kpos = off + row
            sel = jnp.logical_or(s > thr, jnp.logical_and(s == thr, kpos <= jlim))
            bias = jnp.where(sel, 0.0, MASKED)
            bias_u = jnp.concatenate([bias] * HEADS_PER_UNIT, axis=1)

            def logits(u):
                g = (u * HEADS_PER_UNIT) // GROUP
                kc = k_ref[pl.ds(off, tq), g * HEAD_DIM:(g + 1) * HEAD_DIM]
                qu = jnp.concatenate(
                    [qT_ref[h * HEAD_DIM:(h + 1) * HEAD_DIM, :] for h in unit_heads(u)], axis=1)
                return jnp.dot(kc, qu, preferred_element_type=F32)

            for u in range(min(2, n_units)):
                st_ref[u % 2] = logits(u)
            for u in range(n_units):
                g = (u * HEADS_PER_UNIT) // GROUP
                x = st_ref[u % 2] + bias_u
                if exact:
                    m_old = m_ref[u]
                    m_new = jnp.maximum(m_old, jnp.max(x, axis=0, keepdims=True))
                    alpha = jnp.exp2(m_old - m_new)
                    m_ref[u] = m_new
                else:
                    m_new = m_ref[u]
                p = jnp.exp2(x - m_new)
                psum = jnp.sum(p, axis=0, keepdims=True)
                pv = jnp.dot(vT_ref[g * HEAD_DIM:(g + 1) * HEAD_DIM, pl.ds(off, tq)],
                             p.astype(BF16), preferred_element_type=F32)
                if exact:
                    l_ref[u] = alpha * l_ref[u] + psum
                    acc_ref[u] = acc_ref[u] * alpha + pv
                else:
                    l_ref[u] = l_ref[u] + psum
                    acc_ref[u] = acc_ref[u] + pv
                if u + 2 < n_units:
                    st_ref[u % 2] = logits(u + 2)
            return carry

        lax.fori_loop(0, nck, attn_chunk, 0)

    attention_pass(exact=False)
    l_all = l_ref[...]
    in_range = jnp.logical_and(jnp.min(l_all) >= SUM_FLOOR, jnp.max(l_all) < jnp.inf)

    @pl.when(jnp.logical_not(in_range))
    def _():
        attention_pass(exact=True)

    for u in range(n_units):
        ou = acc_ref[u] * (1.0 / l_ref[u])
        for j, h in enumerate(unit_heads(u)):
            o_ref[:, h * HEAD_DIM:(h + 1) * HEAD_DIM] = (
                ou[:, j * tq:(j + 1) * tq].T.astype(o_ref.dtype))


def _prompt_attention(qT, qiT, wT, ki, kb, vT, tq):
    seq = kb.shape[0]
    topk = min(TOPK_MAX, seq // 4)
    const = lambda shape: pl.BlockSpec(shape, lambda i: (0, 0), pipeline_mode=pl.Buffered(1))
    return pl.pallas_call(
        functools.partial(_prompt_attn_kernel, tq=tq, topk=topk, seq=seq),
        grid=(seq // tq,),
        in_specs=[pl.BlockSpec((N_HEADS * HEAD_DIM, tq), lambda i: (0, i)),
                  pl.BlockSpec((IDX_HEADS * IDX_DIM, tq), lambda i: (0, i)),
                  pl.BlockSpec((IDX_HEADS, tq), lambda i: (0, i)),
                  const((seq, IDX_DIM)),
                  const((seq, N_KV_HEADS * HEAD_DIM)),
                  const((N_KV_HEADS * HEAD_DIM, seq))],
        out_specs=pl.BlockSpec((tq, N_HEADS * HEAD_DIM), lambda i: (i, 0)),
        out_shape=jax.ShapeDtypeStruct((seq, N_HEADS * HEAD_DIM), BF16),
        scratch_shapes=[pltpu.VMEM((seq, tq), F32),
                        pltpu.VMEM((N_HEADS // HEADS_PER_UNIT, HEAD_DIM, HEADS_PER_UNIT * tq), F32),
                        pltpu.VMEM((N_HEADS // HEADS_PER_UNIT, 1, HEADS_PER_UNIT * tq), F32),
                        pltpu.VMEM((N_HEADS // HEADS_PER_UNIT, 1, HEADS_PER_UNIT * tq), F32),
                        pltpu.VMEM((2, tq, HEADS_PER_UNIT * tq), F32),
                        pltpu.SMEM((N_KV_HEADS,), F32)],
        compiler_params=_cparams(1),
        name="prompt_attention",
    )(qT, qiT, wT, ki, kb, vT)


def _sample_score_kernel(pt_ref, qi_ref, w_ref, kin_ref, *rest, n_seq, n_pages, page, t_new):
    ki_refs = rest[:n_seq * n_pages]
    sc_ref = rest[n_seq * n_pages]
    del pt_ref
    nt = (((1,), (1,)), ((), ()))
    lane = lax.broadcasted_iota(jnp.int32, (1, page), 1)
    trow = lax.broadcasted_iota(jnp.int32, (t_new, 1), 0)
    for s in range(n_seq):
        qi = qi_ref[s]
        wb = jnp.broadcast_to(w_ref[s] * IDX_SCALE, (IDX_HEADS * t_new, page))

        def head_sum(d, wb=wb):
            r = jnp.maximum(d, 0.0) * wb
            return jnp.sum(r.reshape(IDX_HEADS, t_new, page), axis=0)

        for p in range(n_pages):
            sc_ref[s, :, p * page:(p + 1) * page] = head_sum(
                jnp.dot(qi, ki_refs[s * n_pages + p][...].astype(BF16),
                        preferred_element_type=F32))
        kin = jnp.concatenate(
            [kin_ref[s], jnp.zeros((page - t_new, IDX_DIM), F32)], axis=0).astype(BF16)
        s_new = head_sum(lax.dot_general(qi, kin, nt, preferred_element_type=F32))
        sc_ref[s, :, n_pages * page:] = jnp.where(lane <= trow, s_new, -jnp.inf)


def _select_kernel(sc_ref, thr_ref, jlim_ref, *, t_new, past, topk):
    rows, total = sc_ref.shape
    kf = float(topk)
    lane = lax.broadcasted_iota(jnp.int32, (1, total), 1)
    trow = lax.broadcasted_iota(jnp.int32, (rows, 1), 0) % t_new

    def count_where(pred):
        return jnp.sum(jnp.where(pred(sc_ref[...], lane), 1.0, 0.0), axis=1, keepdims=True)

    sc_all = sc_ref[...]
    mx = jnp.max(sc_all, axis=1, keepdims=True)
    mn = jnp.min(jnp.where(sc_all == -jnp.inf, jnp.inf, sc_all), axis=1, keepdims=True)
    need = (past + trow + 1) > topk
    thr, tie = _bisect_threshold(
        lambda t: count_where(lambda s, kp: s >= t), mn, mx, need, kf)
    thr = jnp.where(need, thr, -jnp.inf)

    any_tie = jnp.max(tie) > 0.5
    nbits = max(1, (total - 1).bit_length())

    def tie_search(_):
        c_gt = count_where(lambda s, kp: s > thr)
        r = kf - c_gt

        def body(b, j):
            cand = j + jnp.left_shift(jnp.int32(1), nbits - 1 - b)
            cnt = count_where(lambda s, kp: jnp.logical_and(s == thr, kp < cand))
            return jnp.where(cnt < r, cand, j)
        j = lax.fori_loop(0, nbits, body, jnp.zeros((rows, 1), jnp.int32))
        return jnp.where(tie > 0.5, j, jnp.int32(total))

    jlim = lax.cond(any_tie, tie_search,
                    lambda _: jnp.full((rows, 1), total, jnp.int32), 0)
    thr_ref[...] = thr
    jlim_ref[...] = jnp.where(need, jlim, -1)


def _sample_attn_kernel(pt_ref, sc_ref, thr_ref, jlim_ref, q_ref, kn_ref, vn_ref, *rest,
                        n_pages, page, t_new):
    k_refs = rest[:n_pages]
    v_refs = rest[n_pages:2 * n_pages]
    o_ref = rest[2 * n_pages]
    s_ref = rest[2 * n_pages + 1]
    del pt_ref
    nchunk = n_pages + 1
    nt = (((1,), (1,)), ((), ()))

    def pad_rows(x, rows):
        return jnp.concatenate(
            [x, jnp.zeros((rows - x.shape[0], x.shape[1]), x.dtype)], axis=0).astype(BF16)

    lane = lax.broadcasted_iota(jnp.int32, (1, page), 1)
    thr = thr_ref[...]
    jlim = jlim_ref[...]

    q = q_ref[...]
    rows_g = GROUP * t_new

    def head_rows(ref, g, n_keys):
        return ref[pl.ds(g, n_keys, stride=N_KV_HEADS), :]

    def logits_chunk(c, k_of):
        s = sc_ref[:, c * page:(c + 1) * page]
        kpos = c * page + lane
        sel = jnp.logical_or(s > thr, jnp.logical_and(s == thr, kpos <= jlim))
        bias = jnp.where(sel, 0.0, MASKED)
        bias4 = jnp.concatenate([bias] * GROUP, axis=0)
        for g in range(N_KV_HEADS):
            st = lax.dot_general(q[g * rows_g:(g + 1) * rows_g, :], k_of(g), nt,
                                 preferred_element_type=F32)
            s_ref[g * rows_g:(g + 1) * rows_g, c * page:(c + 1) * page] = st + bias4

    for p in range(n_pages):
        logits_chunk(p, lambda g, p=p: head_rows(k_refs[p], g, page).astype(BF16))
    logits_chunk(n_pages, lambda g: pad_rows(head_rows(kn_ref, g, t_new), page))

    s_all = s_ref[...]
    m = jnp.max(s_all, axis=1, keepdims=True)
    pr = jnp.exp2(s_all - m)
    l = jnp.sum(pr, axis=1, keepdims=True)
    s_ref[...] = pr

    accs = [jnp.zeros((rows_g, HEAD_DIM), F32) for _ in range(N_KV_HEADS)]
    for c in range(nchunk):
        for g in range(N_KV_HEADS):
            vc = (head_rows(v_refs[c], g, page).astype(BF16) if c < n_pages
                  else pad_rows(head_rows(vn_ref, g, t_new), page))
            pc = s_ref[g * rows_g:(g + 1) * rows_g, c * page:(c + 1) * page].astype(BF16)
            accs[g] = accs[g] + jnp.dot(pc, vc, preferred_element_type=F32)

    inv = 1.0 / l
    for g in range(N_KV_HEADS):
        og = accs[g] * inv[g * rows_g:(g + 1) * rows_g, :]
        for j in range(GROUP):
            h = g * GROUP + j
            o_ref[:, h * HEAD_DIM:(h + 1) * HEAD_DIM] = (
                og[j * t_new:(j + 1) * t_new, :].astype(o_ref.dtype))


def _sample_attention(layer, page_table, qi_s, w_s, q_s, ki_new, k_new, v_new,
                      cache_ki, cache_k, cache_v):
    db, n_pages = page_table.shape
    page = cache_k.shape[2]
    t_new = ki_new.shape[1]
    past = n_pages * page
    topk = min(TOPK_MAX, (past + t_new) // 4)
    rows = N_HEADS * t_new
    kv_rows = page * N_KV_HEADS
    new_rows = t_new * N_KV_HEADS
    total = (n_pages + 1) * page

    def per_seq(shape):
        return pl.BlockSpec((None,) + shape, lambda b, pt: (b, 0, 0))

    def paged(shape, p):
        return pl.BlockSpec((None, None) + shape,
                            lambda b, pt, p=p: (layer, pt[b, p], 0, 0))

    ck = cache_k.reshape(cache_k.shape[0], cache_k.shape[1], kv_rows, HEAD_DIM)
    cv = cache_v.reshape(cache_v.shape[0], cache_v.shape[1], kv_rows, HEAD_DIM)
    ckit = jnp.swapaxes(cache_ki, 2, 3)

    n_seq = SCORE_SEQS if db % SCORE_SEQS == 0 else 1

    def seq_group(shape):
        return pl.BlockSpec((n_seq,) + shape, lambda b, pt: (b, 0, 0))

    def paged_of(shape, s, p):
        return pl.BlockSpec((None, None) + shape,
                            lambda b, pt, s=s, p=p: (layer, pt[b * n_seq + s, p], 0, 0))

    scores = pl.pallas_call(
        functools.partial(_sample_score_kernel, n_seq=n_seq, n_pages=n_pages, page=page,
                          t_new=t_new),
        grid_spec=pltpu.PrefetchScalarGridSpec(
            num_scalar_prefetch=1,
            grid=(db // n_seq,),
            in_specs=[seq_group((rows, IDX_DIM)), seq_group((rows, 1)),
                      seq_group((t_new, IDX_DIM))]
            + [paged_of((IDX_DIM, page), s, p) for s in range(n_seq) for p in range(n_pages)],
            out_specs=seq_group((t_new, total)),
        ),
        out_shape=jax.ShapeDtypeStruct((db, t_new, total), F32),
        compiler_params=_cparams(1),
        name="sample_scores",
    )(page_table, qi_s, w_s, ki_new, *([ckit] * (n_seq * n_pages)))

    thr, jlim = pl.pallas_call(
        functools.partial(_select_kernel, t_new=t_new, past=past, topk=topk),
        out_shape=[jax.ShapeDtypeStruct((db * t_new, 1), F32),
                   jax.ShapeDtypeStruct((db * t_new, 1), jnp.int32)],
        compiler_params=pltpu.CompilerParams(vmem_limit_bytes=VMEM_LIMIT),
        name="sample_select",
    )(scores.reshape(db * t_new, total))

    return pl.pallas_call(
        functools.partial(_sample_attn_kernel, n_pages=n_pages, page=page, t_new=t_new),
        grid_spec=pltpu.PrefetchScalarGridSpec(
            num_scalar_prefetch=1,
            grid=(db,),
            in_specs=[per_seq((t_new, total)), per_seq((t_new, 1)), per_seq((t_new, 1)),
                      per_seq((rows, HEAD_DIM)), per_seq((new_rows, HEAD_DIM)),
                      per_seq((new_rows, HEAD_DIM))]
            + [paged((kv_rows, HEAD_DIM), p) for p in range(n_pages)]
            + [paged((kv_rows, HEAD_DIM), p) for p in range(n_pages)],
            out_specs=pl.BlockSpec((None, t_new, N_HEADS * HEAD_DIM), lambda b, pt: (b, 0, 0)),
            scratch_shapes=[pltpu.VMEM((rows, total), F32)],
        ),
        out_shape=jax.ShapeDtypeStruct((db, t_new, N_HEADS * HEAD_DIM), F32),
        compiler_params=_cparams(1),
        name="sample_attention",
    )(page_table, scores, thr.reshape(db, t_new, 1), jlim.reshape(db, t_new, 1), q_s,
      k_new, v_new, *([ck] * n_pages), *([cv] * n_pages))


def _conv_kernel(u_ref, halo_ref, cb_ref, w_ref, o_ref, *, seq_len, tm):
    i = pl.program_id(0)
    u = u_ref[...]
    halo = halo_ref[...]
    r = lax.broadcasted_iota(jnp.int32, (tm, 1), 0)
    if seq_len >= tm:
        t = r
        halo = jnp.where(i > 0, halo, 0.0)
        h7 = halo[SUBLANES - 1:SUBLANES, :]
        h6 = halo[SUBLANES - 2:SUBLANES - 1, :]
        pre1 = jnp.broadcast_to(h7, u.shape)
        pre2 = jnp.where(r == 0, h6, h7)
    else:
        t = r % seq_len
        pre2 = halo
        pre1 = pltpu.roll(halo, tm - 1, 0)
    u1 = jnp.where(t >= 1, pltpu.roll(u, 1, 0), pre1)
    u2 = jnp.where(t >= 2, pltpu.roll(u, 2, 0), pre2)
    y = w_ref[0:1, :] * u2 + w_ref[1:2, :] * u1 + w_ref[2:3, :] * u
    o_ref[...] = (cb_ref[...] * y).astype(o_ref.dtype)


def _conv_branch(u, cb, w_conv, history, seq_len):
    t, c = u.shape
    tm = _row_tile(t)
    if history is None:
        halo_arr = u
        halo_spec = pl.BlockSpec(
            (SUBLANES, c), lambda i: (jnp.maximum(i * (tm // SUBLANES) - 1, 0), 0))
    else:
        halo_arr = history
        halo_spec = pl.BlockSpec((tm, c), lambda i: (i, 0))
    return pl.pallas_call(
        functools.partial(_conv_kernel, seq_len=seq_len, tm=tm),
        grid=(t // tm,),
        in_specs=[pl.BlockSpec((tm, c), lambda i: (i, 0)), halo_spec,
                  pl.BlockSpec((tm, c), lambda i: (i, 0)),
                  pl.BlockSpec((CONV_WIDTH, c), lambda i: (0, 0))],
        out_specs=pl.BlockSpec((tm, c), lambda i: (i, 0)),
        out_shape=jax.ShapeDtypeStruct((t, c), BF16),
        compiler_params=_cparams(1),
        name="short_conv",
    )(u, halo_arr, cb, w_conv)


def _merge_kernel(a_ref, c_ref, wa_ref, wc_ref, ga_ref, gb_ref, o_ref):
    a = jnp.dot(a_ref[...], wa_ref[...].astype(BF16), preferred_element_type=F32)
    c = jnp.dot(c_ref[...], wc_ref[...].astype(BF16), preferred_element_type=F32)
    o_ref[...] = (ga_ref[...] * a + gb_ref[...] * c).astype(o_ref.dtype)


def _merge(a, cpre, w_ao, w_co, layer, gates, tn=512):
    t, d = a.shape[0], w_ao.shape[2]
    tm = _row_tile(t, BIG_ROW_TILE)
    nj = d // tn
    return pl.pallas_call(
        _merge_kernel,
        grid=(t // tm, nj),
        in_specs=[pl.BlockSpec((tm, a.shape[1]), lambda i, j: (i, 0)),
                  pl.BlockSpec((tm, cpre.shape[1]), lambda i, j: (i, 0)),
                  _layer_block(layer, (w_ao.shape[1], tn), lambda i, j: (0, j)),
                  _layer_block(layer, (w_co.shape[1], tn), lambda i, j: (0, j)),
                  pl.BlockSpec((tm, tn), lambda i, j: (i, j)),
                  pl.BlockSpec((tm, tn), lambda i, j: (i, j + nj))],
        out_specs=pl.BlockSpec((tm, tn), lambda i, j: (i, j)),
        out_shape=jax.ShapeDtypeStruct((t, d), BF16),
        compiler_params=_cparams(2),
        name="gated_merge",
    )(a, cpre, w_ao, w_co, gates, gates)


def _out_proj_kernel(m_ref, w_ref, x_ref, g_ref, x1_ref, xn_ref):
    h = jnp.dot(m_ref[...], w_ref[...].astype(BF16), preferred_element_type=F32)
    x1 = x_ref[...] + h
    x1_ref[...] = x1
    xn_ref[...] = _rms_body(x1, g_ref[...]).astype(xn_ref.dtype)


def _out_proj(merged, w_o, layer, x, g):
    t, d = x.shape
    tm = _row_tile(t)
    return pl.pallas_call(
        _out_proj_kernel,
        grid=(t // tm,),
        in_specs=[pl.BlockSpec((tm, d), lambda i: (i, 0)),
                  _layer_block(layer, (d, d), lambda i: (0, 0), pipeline_mode=pl.Buffered(1)),
                  pl.BlockSpec((tm, d), lambda i: (i, 0)),
                  pl.BlockSpec((1, d), lambda i: (0, 0))],
        out_specs=[pl.BlockSpec((tm, d), lambda i: (i, 0)),
                   pl.BlockSpec((tm, d), lambda i: (i, 0))],
        out_shape=[jax.ShapeDtypeStruct((t, d), F32), jax.ShapeDtypeStruct((t, d), BF16)],
        compiler_params=_cparams(1),
        name="out_proj",
    )(merged, w_o, x, g.reshape(1, d))


def _ffn_in_kernel(x_ref, wg_ref, wu_ref, o_ref):
    x = x_ref[...]
    g = jnp.dot(x, wg_ref[...].astype(BF16), preferred_element_type=F32)
    u = jnp.dot(x, wu_ref[...].astype(BF16), preferred_element_type=F32)
    o_ref[...] = (jax.nn.silu(g) * u).astype(o_ref.dtype)


def _ffn_in(xn, w_fi, layer, tn=512):
    t, d = xn.shape
    d_ff = w_fi.shape[2] // 2
    tm = _row_tile(t, BIG_ROW_TILE)
    nj = d_ff // tn
    return pl.pallas_call(
        _ffn_in_kernel,
        grid=(t // tm, nj),
        in_specs=[pl.BlockSpec((tm, d), lambda i, j: (i, 0)),
                  _layer_block(layer, (d, tn), lambda i, j: (0, j)),
                  _layer_block(layer, (d, tn), lambda i, j: (0, j + nj))],
        out_specs=pl.BlockSpec((tm, tn), lambda i, j: (i, j)),
        out_shape=jax.ShapeDtypeStruct((t, d_ff), BF16),
        compiler_params=_cparams(2),
        name="ffn_in",
    )(xn, w_fi, w_fi)


def _ffn_out_kernel(a_ref, w_ref, x_ref, o_ref):
    o_ref[...] = x_ref[...] + jnp.dot(a_ref[...], w_ref[...], preferred_element_type=F32)


def _ffn_out(act, w_fo, layer, x1, tn=512):
    t, d = x1.shape
    d_ff = act.shape[1]
    tm = _row_tile(t, BIG_ROW_TILE)
    return pl.pallas_call(
        _ffn_out_kernel,
        grid=(t // tm, d // tn),
        in_specs=[pl.BlockSpec((tm, d_ff), lambda i, j: (i, 0)),
                  _layer_block(layer, (d_ff, tn), lambda i, j: (0, j)),
                  pl.BlockSpec((tm, tn), lambda i, j: (i, j))],
        out_specs=pl.BlockSpec((tm, tn), lambda i, j: (i, j)),
        out_shape=jax.ShapeDtypeStruct((t, d), F32),
        compiler_params=_cparams(2),
        name="ffn_out",
    )(act, w_fo, x1)


def _rope_tables(pos, dim, reps):
    inv = ROPE_THETA ** (-jnp.arange(0, dim, 2, dtype=F32) / dim)
    ang = pos.astype(F32)[:, None] * inv[None, :]
    cos, sin = jnp.cos(ang), jnp.sin(ang)
    return (jnp.tile(jnp.concatenate([cos, cos], axis=1), (1, reps)),
            jnp.tile(jnp.concatenate([-sin, sin], axis=1), (1, reps)))


def _split_w_in(w_in, layer, d_model):
    attn, kv, conv = N_HEADS * HEAD_DIM, N_KV_HEADS * HEAD_DIM, d_model // 2
    sizes = (attn, kv, kv, IDX_HEADS * IDX_DIM, IDX_DIM, IDX_HEADS, conv, conv, conv,
             d_model, d_model)
    offs = [sum(sizes[:i]) for i in range(len(sizes))]
    in_place = lambda i, n: (w_in, layer, offs[i], n)
    w = w_in[layer]
    cut = lambda i: w[:, offs[i]:offs[i] + sizes[i]].astype(BF16)
    return dict(q=in_place(0, attn), k=in_place(1, kv), v=in_place(2, kv),
                qi=in_place(3, sizes[3]), kiwi=in_place(4, LANES),
                cb=cut(6), cc=cut(7), ch=cut(8),
                gates=jnp.concatenate([cut(9), cut(10)], axis=1))


def _layer(x, wl, tabs, attn_fn, history, seq_len, attn_cols):
    cos128, sin128, cos64, sin64, cos_kw, sin_kw = tabs
    xn = _rmsnorm(x, wl["norm_mix"], BF16)
    rope128 = functools.partial(_proj_rope_kernel, rope=_rope128)
    rope64 = functools.partial(_proj_rope_kernel, rope=_rope64)
    qlay = "cols" if attn_cols else "rows"
    (q,) = _proj(functools.partial(rope128, scale=Q_SCALE), xn, [wl["q"]],
                 [cos128, sin128], [BF16], 512, "proj_q", (qlay,))
    k, kb = _proj(functools.partial(rope128, scale=1.0), xn, [wl["k"]],
                  [cos128, sin128], [F32, BF16], 512, "proj_k", ("heads", "rows"))
    v, vb = _proj(_proj_plain_kernel, xn, [wl["v"]], [], [F32, BF16], 512, "proj_v",
                  ("heads", qlay))
    (qi,) = _proj(functools.partial(rope64, scale=1.0), xn, [wl["qi"]],
                  [cos64, sin64], [BF16], 512, "proj_qi", (qlay,))
    (kiwi,) = _proj(functools.partial(rope64, scale=1.0), xn, [wl["kiwi"]],
                    [cos_kw, sin_kw], [F32], LANES, "proj_kiwi", ("rows",))
    cb, u = _proj(_proj_conv_kernel, xn, [wl["cb"], wl["cc"], wl["ch"]], [],
                  [F32, F32], 512, "proj_conv")
    (gates,) = _proj(_proj_gate_kernel, xn, [wl["gates"]], [], [BF16], 512, "proj_gates")
    ki = kiwi[:, :IDX_DIM]
    wi = kiwi[:, IDX_DIM:IDX_DIM + IDX_HEADS]

    a = attn_fn(q=q, k=k, kb=kb, v=v, vb=vb, qi=qi, ki=ki, wi=wi)
    cpre = _conv_branch(u, cb, wl["w_conv"], history, seq_len)
    l = wl["layer"]
    merged = _merge(a, cpre, wl["w_attn_out"], wl["w_conv_out"], l, gates)
    x1, xn2 = _out_proj(merged, wl["w_o"], l, x, wl["norm_ffn"])
    act = _ffn_in(xn2, wl["w_ffn_in"], l)
    x2 = _ffn_out(act, wl["w_ffn_out"], l, x1)
    return x2, k, v, ki, u


def kernel(x_prompt, x_sample, cache_k, cache_v, cache_ki, state_conv, page_table,
           norm_mix, w_in, w_conv, w_attn_out, w_conv_out, w_o, norm_ffn, w_ffn_in,
           w_ffn_out, norm_final):
    depth = w_in.shape[0]
    bp, seq, d_model = x_prompt.shape
    db, t_new, _ = x_sample.shape
    page = cache_k.shape[2]
    past = page_table.shape[1] * page
    conv_dim = d_model // 2

    pos_p = jnp.arange(seq)
    pos_s = jnp.tile(past + jnp.arange(t_new), db)

    def tables(pos):
        c128, s128 = _rope_tables(pos, HEAD_DIM, 1)
        c64, s64 = _rope_tables(pos, IDX_DIM, LANES // IDX_DIM)
        keep = jnp.arange(LANES)[None, :] < IDX_DIM
        return (c128, s128, c64, s64, jnp.where(keep, c64, 1.0), jnp.where(keep, s64, 0.0))

    tabs_p, tabs_s = tables(pos_p), tables(pos_s)

    stacked = dict(w_attn_out=w_attn_out, w_conv_out=w_conv_out, w_o=w_o,
                   w_ffn_in=w_ffn_in, w_ffn_out=w_ffn_out.astype(BF16))
    layers = []
    for l in range(depth):
        wl = _split_w_in(w_in, l, d_model)
        wl.update(layer=l, norm_mix=norm_mix[l], norm_ffn=norm_ffn[l], w_conv=w_conv[l],
                  **stacked)
        layers.append(wl)

    tq = min(256, seq)

    def prompt_attn(q, k, kb, v, vb, qi, ki, wi):
        del k, v
        return _prompt_attention(q, qi, wi.T, ki.astype(BF16), kb, vb, tq)

    def head_major(a, width):
        nh = a.shape[1] // width
        return a.reshape(db, t_new, nh, width).transpose(0, 2, 1, 3).reshape(
            db, nh * t_new, width)

    def sample_attn_for(l):
        def fn(q, k, kb, v, vb, qi, ki, wi):
            del kb, vb
            a = _sample_attention(
                l, page_table, head_major(qi, IDX_DIM), head_major(wi, 1),
                head_major(q, HEAD_DIM), ki.reshape(db, t_new, IDX_DIM),
                k.reshape(db, t_new * N_KV_HEADS, HEAD_DIM),
                v.reshape(db, t_new * N_KV_HEADS, HEAD_DIM),
                cache_ki, cache_k, cache_v)
            return a.reshape(db * t_new, N_HEADS * HEAD_DIM).astype(BF16)
        return fn

    def run(x, tabs, make_attn, history_for, seq_len, attn_cols):
        ks, vs, kis, us = [], [], [], []
        for l in range(depth):
            x, k, v, ki, u = _layer(x, layers[l], tabs, make_attn(l),
                                    history_for(l), seq_len, attn_cols)
            ks.append(k)
            vs.append(v)
            kis.append(ki)
            us.append(u)
        return _rmsnorm(x, norm_final, F32), ks, vs, kis, us

    assert bp == 1
    y_p, ks, vs, kis, us = run(x_prompt[0], tabs_p, lambda l: prompt_attn,
                               lambda l: None, seq, True)
    y_prompt = y_p[None]
    new_k_p = jnp.stack(ks).reshape(depth, bp, seq, N_KV_HEADS, HEAD_DIM)
    new_v_p = jnp.stack(vs).reshape(depth, bp, seq, N_KV_HEADS, HEAD_DIM)
    new_ki_p = jnp.stack(kis).reshape(depth, bp, seq, IDX_DIM)
    new_conv_p = jnp.stack([u[seq - (CONV_WIDTH - 1):] for u in us]).reshape(
        depth, bp, CONV_WIDTH - 1, conv_dim)

    def history_for(l):
        h = state_conv[l]
        pad = jnp.zeros((db, t_new - (CONV_WIDTH - 1), conv_dim), F32)
        return jnp.concatenate([h, pad], axis=1).reshape(db * t_new, conv_dim)

    y_s, ks, vs, kis, us = run(x_sample.reshape(db * t_new, d_model), tabs_s,
                               sample_attn_for, history_for, t_new, False)
    y_sample = y_s.reshape(db, t_new, d_model)
    new_k_s = jnp.stack(ks).reshape(depth, db, t_new, N_KV_HEADS, HEAD_DIM)
    new_v_s = jnp.stack(vs).reshape(depth, db, t_new, N_KV_HEADS, HEAD_DIM)
    new_ki_s = jnp.stack(kis).reshape(depth, db, t_new, IDX_DIM)
    new_conv_s = jnp.stack(
        [u.reshape(db, t_new, conv_dim)[:, t_new - (CONV_WIDTH - 1):] for u in us])

    return (y_prompt, y_sample, new_k_p, new_v_p, new_ki_p, new_conv_p,
            new_k_s, new_v_s, new_ki_s, new_conv_s)
```

```python
import functools

import jax
import jax.numpy as jnp
from jax import lax
from jax.experimental import pallas as pl
from jax.experimental.pallas import tpu as pltpu

N_HEADS = 16
N_KV_HEADS = 4
HEAD_DIM = 128
GROUP = N_HEADS // N_KV_HEADS
IDX_HEADS = 16
IDX_DIM = 64
TOPK_MAX = 256
CONV_WIDTH = 3
ROPE_THETA = 10000.0
NORM_EPS = 1e-6
IDX_SCALE = IDX_HEADS ** -0.5 * IDX_DIM ** -0.5
LOG2_E = 1.4426950408889634
Q_SCALE = HEAD_DIM ** -0.5 * LOG2_E

LANES = 128
SUBLANES = 8
VMEM_LIMIT = 48 * 1024 * 1024
MASKED = -1e30
MAX_BISECT = 200
HEADS_PER_UNIT = 4
SUM_FLOOR = 2.0 ** -100
SCORE_SEQS = 2
COUNT_ROWS = 32
BIG_ROW_TILE = 1024

F32 = jnp.float32
BF16 = jnp.bfloat16


def _cparams(n_grid):
    return pltpu.CompilerParams(
        dimension_semantics=("arbitrary",) * n_grid, vmem_limit_bytes=VMEM_LIMIT)


def _row_tile(t, cap=512):
    return min(cap, t)


def _layer_block(layer, block, index_map, **kw):
    return pl.BlockSpec((None,) + block, lambda *a: (layer,) + index_map(*a), **kw)


def _rms_body(x, g):
    y = x * lax.rsqrt(jnp.mean(x * x, axis=-1, keepdims=True) + NORM_EPS)
    return y * g


def _rmsnorm_kernel(x_ref, g_ref, o_ref):
    o_ref[...] = _rms_body(x_ref[...], g_ref[...]).astype(o_ref.dtype)


def _rmsnorm(x, g, out_dtype):
    t, d = x.shape
    tm = _row_tile(t)
    return pl.pallas_call(
        _rmsnorm_kernel,
        grid=(t // tm,),
        in_specs=[pl.BlockSpec((tm, d), lambda i: (i, 0)),
                  pl.BlockSpec((1, d), lambda i: (0, 0))],
        out_specs=pl.BlockSpec((tm, d), lambda i: (i, 0)),
        out_shape=jax.ShapeDtypeStruct((t, d), out_dtype),
        compiler_params=_cparams(1),
        name="rmsnorm",
    )(x, g.reshape(1, d))


def _rope128(z, cos, sin):
    outs = []
    for c in range(z.shape[1] // LANES):
        zc = z[:, c * LANES:(c + 1) * LANES]
        outs.append(zc * cos + pltpu.roll(zc, HEAD_DIM // 2, 1) * sin)
    return outs[0] if len(outs) == 1 else jnp.concatenate(outs, axis=1)


def _rope64(z, cos, sin):
    lane = lax.broadcasted_iota(jnp.int32, (1, LANES), 1)
    first_half = (lane % IDX_DIM) < (IDX_DIM // 2)
    outs = []
    for c in range(z.shape[1] // LANES):
        zc = z[:, c * LANES:(c + 1) * LANES]
        rot = jnp.where(first_half,
                        pltpu.roll(zc, LANES - IDX_DIM // 2, 1),
                        pltpu.roll(zc, IDX_DIM // 2, 1))
        outs.append(zc * cos + rot * sin)
    return outs[0] if len(outs) == 1 else jnp.concatenate(outs, axis=1)


def _rope_cols(zt, cos_t, sin_t, width):
    half = width // 2
    outs = []
    for c in range(zt.shape[0] // LANES):
        blk = zt[c * LANES:(c + 1) * LANES, :]
        swapped = jnp.concatenate(
            [blk[b + half:b + width, :] if first else blk[b:b + half, :]
             for b in range(0, LANES, width) for first in (True, False)], axis=0)
        outs.append(blk * cos_t + swapped * sin_t)
    return outs[0] if len(outs) == 1 else jnp.concatenate(outs, axis=0)


def _store_all(z, o_refs, layouts):
    zt = z.T if "cols" in layouts else None
    for o, layout in zip(o_refs, layouts):
        if layout == "cols":
            o[...] = zt.astype(o.dtype)
        elif layout == "heads":
            n_heads = z.shape[1] // HEAD_DIM
            for g in range(n_heads):
                o[pl.ds(g, z.shape[0], stride=n_heads), :] = (
                    z[:, g * HEAD_DIM:(g + 1) * HEAD_DIM].astype(o.dtype))
        else:
            o[...] = z.astype(o.dtype)


def _proj_rope_kernel(x_ref, w_ref, cos_ref, sin_ref, *o_refs, rope, scale, layouts):
    z = jnp.dot(x_ref[...], w_ref[...], preferred_element_type=F32)
    if scale != 1.0:
        z = z * scale
    if layouts == ("cols",):
        width = HEAD_DIM if rope is _rope128 else IDX_DIM
        zt = _rope_cols(z.T, cos_ref[...], sin_ref[...], width)
        o_refs[0][...] = zt.astype(o_refs[0].dtype)
    else:
        _store_all(rope(z, cos_ref[...], sin_ref[...]), o_refs, layouts)


def _proj_plain_kernel(x_ref, w_ref, *o_refs, layouts):
    z = jnp.dot(x_ref[...], w_ref[...], preferred_element_type=F32)
    _store_all(z, o_refs, layouts)


def _proj_conv_kernel(x_ref, wb_ref, wc_ref, wh_ref, cb_ref, u_ref):
    x = x_ref[...]
    cb_ref[...] = jnp.dot(x, wb_ref[...], preferred_element_type=F32)
    cc = jnp.dot(x, wc_ref[...], preferred_element_type=F32)
    ch = jnp.dot(x, wh_ref[...], preferred_element_type=F32)
    u_ref[...] = cc * ch


def _proj(kernel, xn, ws, tables, out_dtypes, tn, name, layouts=None):
    t, k = xn.shape
    n = ws[0].shape[1]
    tm = _row_tile(t, BIG_ROW_TILE)
    tn = min(tn, n)
    if layouts is None:
        layouts = ("rows",) * len(out_dtypes)
    else:
        kernel = functools.partial(kernel, layouts=layouts)
    n_heads = n // HEAD_DIM

    def out_spec(layout):
        if layout == "cols":
            return pl.BlockSpec((tn, tm), lambda i, j: (j, i))
        if layout == "heads":
            assert tn == n
            return pl.BlockSpec((tm * n_heads, HEAD_DIM), lambda i, j: (i, 0))
        return pl.BlockSpec((tm, tn), lambda i, j: (i, j))

    def out_dims(layout):
        return {"cols": (n, t), "heads": (t * n_heads, HEAD_DIM)}.get(layout, (t, n))

    in_specs = [pl.BlockSpec((tm, k), lambda i, j: (i, 0))]
    in_specs += [pl.BlockSpec((k, tn), lambda i, j: (0, j)) for _ in ws]
    if tables and layouts == ("cols",):
        tables = [tb.T for tb in tables]
        in_specs += [pl.BlockSpec((LANES, tm), lambda i, j: (0, i)) for _ in tables]
    else:
        in_specs += [pl.BlockSpec((tm, LANES), lambda i, j: (i, 0)) for _ in tables]
    out_specs = [out_spec(layout) for layout in layouts]
    out_shape = [jax.ShapeDtypeStruct(out_dims(layout), dt)
                 for dt, layout in zip(out_dtypes, layouts)]
    outs = pl.pallas_call(
        kernel,
        grid=(t // tm, n // tn),
        in_specs=in_specs,
        out_specs=out_specs,
        out_shape=out_shape,
        compiler_params=_cparams(2),
        name=name,
    )(xn, *ws, *tables)
    return outs


def _bisect_threshold(count_ge, lo0, hi0, need, kf, steps_per_check=1):
    big = jnp.maximum(jnp.abs(hi0) * 1e-6, 1e-30)
    hi0 = hi0 + big
    done0 = jnp.where(need, 0.0, 1.0)
    zeros = jnp.zeros_like(lo0)

    def step(lo, hi, thr, done, tie):
        mid = 0.5 * lo + 0.5 * hi
        cnt = count_ge(mid)
        live = done < 0.5
        collapsed = jnp.logical_or(mid <= lo, mid >= hi)
        found = jnp.logical_and(cnt == kf, jnp.logical_not(collapsed))
        fin = jnp.logical_and(live, jnp.logical_or(found, collapsed))
        thr = jnp.where(jnp.logical_and(live, found), mid, thr)
        thr = jnp.where(jnp.logical_and(live, collapsed), lo, thr)
        tie = jnp.where(jnp.logical_and(live, collapsed), 1.0, tie)
        done = jnp.where(fin, 1.0, done)
        go_up = jnp.logical_and(live, cnt > kf)
        go_dn = jnp.logical_and(live, cnt < kf)
        lo = jnp.where(go_up, mid, lo)
        hi = jnp.where(go_dn, mid, hi)
        return lo, hi, thr, done, tie

    def cond(c):
        it, pending = c[0], c[1]
        return jnp.logical_and(it < MAX_BISECT, pending > 0)

    def body(c):
        it, _, lo, hi, thr, done, tie = c
        for _ in range(steps_per_check):
            lo, hi, thr, done, tie = step(lo, hi, thr, done, tie)
        pending = (jnp.min(done) < 0.5).astype(jnp.int32)
        return it + steps_per_check, pending, lo, hi, thr, done, tie

    _, _, lo, _, thr, done, tie = lax.while_loop(
        cond, body, (jnp.int32(0), jnp.int32(1), lo0, hi0, lo0, done0, zeros))
    thr = jnp.where(done > 0.5, thr, lo)
    return thr, tie


def _prompt_attn_kernel(qT_ref, qiT_ref, wT_ref, ki_ref, k_ref, vT_ref, o_ref,
                        sc_ref, acc_ref, m_ref, l_ref, st_ref, knorm_ref, *, tq, topk, seq):
    i = pl.program_id(0)
    nck = i + 1
    qpos = i * tq + lax.broadcasted_iota(jnp.int32, (1, tq), 1)
    row = lax.broadcasted_iota(jnp.int32, (tq, 1), 0)
    kf = float(topk)

    def score_chunk(c, carry):
        mx, mn = carry
        off = pl.multiple_of(c * tq, tq)
        kic = ki_ref[pl.ds(off, tq), :]
        acc = jnp.zeros((tq, tq), F32)
        for h in range(IDX_HEADS):
            d = jnp.dot(kic, qiT_ref[h * IDX_DIM:(h + 1) * IDX_DIM, :],
                        preferred_element_type=F32)
            acc = acc + jnp.maximum(d, 0.0) * (wT_ref[h:h + 1, :] * IDX_SCALE)
        valid = (off + row) <= qpos
        sc_ref[pl.ds(off, tq), :] = jnp.where(valid, acc, -jnp.inf)
        mx = jnp.maximum(mx, jnp.max(jnp.where(valid, acc, -jnp.inf), axis=0, keepdims=True))
        mn = jnp.minimum(mn, jnp.min(jnp.where(valid, acc, jnp.inf), axis=0, keepdims=True))
        return mx, mn

    mx, mn = lax.fori_loop(
        0, nck, score_chunk,
        (jnp.full((1, tq), -jnp.inf, F32), jnp.full((1, tq), jnp.inf, F32)))

    def count_where(pred):
        def body(c, acc):
            off = pl.multiple_of(c * tq, tq)
            ind = jnp.where(pred(sc_ref[pl.ds(off, tq), :], off + row), 1.0, 0.0)
            return acc + jnp.sum(ind.reshape(tq // COUNT_ROWS, COUNT_ROWS, tq), axis=0)
        acc = lax.fori_loop(0, nck, body, jnp.zeros((COUNT_ROWS, tq), F32))
        return jnp.sum(acc, axis=0, keepdims=True)

    need = (qpos + 1) > topk
    thr, tie = _bisect_threshold(
        lambda t: count_where(lambda s, kp: s >= t), mn, mx, need, kf, steps_per_check=2)
    thr = jnp.where(need, thr, -jnp.inf)

    any_tie = jnp.max(tie) > 0.5
    nbits = max(1, (seq - 1).bit_length())

    def tie_search(_):
        c_gt = count_where(lambda s, kp: s > thr)
        r = kf - c_gt

        def body(b, j):
            cand = j + jnp.left_shift(jnp.int32(1), nbits - 1 - b)
            cnt = count_where(lambda s, kp: jnp.logical_and(s == thr, kp < cand))
            return jnp.where(cnt < r, cand, j)
        j = lax.fori_loop(0, nbits, body, jnp.zeros((1, tq), jnp.int32))
        return jnp.where(tie > 0.5, j, jnp.int32(seq))

    jlim = lax.cond(any_tie, tie_search,
                    lambda _: jnp.full((1, tq), seq, jnp.int32), 0)
    jlim = jnp.where(need, jlim, -1)

    n_units = N_HEADS // HEADS_PER_UNIT

    @pl.when(i == 0)
    def _():
        rows = min(512, seq)
        for g in range(N_KV_HEADS):
            def body(c, mx):
                kc = k_ref[pl.ds(pl.multiple_of(c * rows, rows), rows),
                           g * HEAD_DIM:(g + 1) * HEAD_DIM].astype(F32)
                return jnp.maximum(mx, jnp.max(jnp.sum(kc * kc, axis=1, keepdims=True)))
            knorm_ref[g] = jnp.sqrt(lax.fori_loop(0, seq // rows, body, jnp.float32(0.0)))

    def unit_heads(u):
        return range(u * HEADS_PER_UNIT, (u + 1) * HEADS_PER_UNIT)

    def attention_pass(exact):
        l_ref[...] = jnp.zeros(l_ref.shape, F32)
        acc_ref[...] = jnp.zeros(acc_ref.shape, F32)
        if exact:
            m_ref[...] = jnp.full(m_ref.shape, MASKED, F32)
        else:
            for u in range(n_units):
                norms = []
                for h in unit_heads(u):
                    qh = qT_ref[h * HEAD_DIM:(h + 1) * HEAD_DIM, :].astype(F32)
                    norms.append(jnp.sqrt(jnp.sum(qh * qh, axis=0, keepdims=True))
                                 * knorm_ref[h // GROUP])
                m_ref[u] = jnp.concatenate(norms, axis=1)

        def attn_chunk(c, carry):
            off = pl.multiple_of(c * tq, tq)
            s = sc_ref[pl.ds(off, tq), :]
            kpos = off + row
            sel = jnp.logical_or(s > thr, jnp.logical_and(s == thr, kpos <= jlim))
            bias = jnp.where(sel, 0.0, MASKED)
            bias_u = jnp.concatenate([bias] * HEADS_PER_UNIT, axis=1)

            def logits(u):
                g = (u * HEADS_PER_UNIT) // GROUP
                kc = k_ref[pl.ds(off, tq), g * HEAD_DIM:(g + 1) * HEAD_DIM]
                qu = jnp.concatenate(
                    [qT_ref[h * HEAD_DIM:(h + 1) * HEAD_DIM, :] for h in unit_heads(u)], axis=1)
                return jnp.dot(kc, qu, preferred_element_type=F32)

            for u in range(min(2, n_units)):
                st_ref[u % 2] = logits(u)
            for u in range(n_units):
                g = (u * HEADS_PER_UNIT) // GROUP
                x = st_ref[u % 2] + bias_u
                if exact:
                    m_old = m_ref[u]
                    m_new = jnp.maximum(m_old, jnp.max(x, axis=0, keepdims=True))
                    alpha = jnp.exp2(m_old - m_new)
                    m_ref[u] = m_new
                else:
                    m_new = m_ref[u]
                p = jnp.exp2(x - m_new)
                psum = jnp.sum(p, axis=0, keepdims=True)
                pv = jnp.dot(vT_ref[g * HEAD_DIM:(g + 1) * HEAD_DIM, pl.ds(off, tq)],
                             p.astype(BF16), preferred_element_type=F32)
                if exact:
                    l_ref[u] = alpha * l_ref[u] + psum
                    acc_ref[u] = acc_ref[u] * alpha + pv
                else:
                    l_ref[u] = l_ref[u] + psum
                    acc_ref[u] = acc_ref[u] + pv
                if u + 2 < n_units:
                    st_ref[u % 2] = logits(u + 2)
            return carry

        lax.fori_loop(0, nck, attn_chunk, 0)

    attention_pass(exact=False)
    l_all = l_ref[...]
    in_range = jnp.logical_and(jnp.min(l_all) >= SUM_FLOOR, jnp.max(l_all) < jnp.inf)

    @pl.when(jnp.logical_not(in_range))
    def _():
        attention_pass(exact=True)

    for u in range(n_units):
        ou = acc_ref[u] * (1.0 / l_ref[u])
        for j, h in enumerate(unit_heads(u)):
            o_ref[:, h * HEAD_DIM:(h + 1) * HEAD_DIM] = (
                ou[:, j * tq:(j + 1) * tq].T.astype(o_ref.dtype))


def _prompt_attention(qT, qiT, wT, ki, kb, vT, tq):
    seq = kb.shape[0]
    topk = min(TOPK_MAX, seq // 4)
    const = lambda shape: pl.BlockSpec(shape, lambda i: (0, 0), pipeline_mode=pl.Buffered(1))
    return pl.pallas_call(
        functools.partial(_prompt_attn_kernel, tq=tq, topk=topk, seq=seq),
        grid=(seq // tq,),
        in_specs=[pl.BlockSpec((N_HEADS * HEAD_DIM, tq), lambda i: (0, i)),
                  pl.BlockSpec((IDX_HEADS * IDX_DIM, tq), lambda i: (0, i)),
                  pl.BlockSpec((IDX_HEADS, tq), lambda i: (0, i)),
                  const((seq, IDX_DIM)),
                  const((seq, N_KV_HEADS * HEAD_DIM)),
                  const((N_KV_HEADS * HEAD_DIM, seq))],
        out_specs=pl.BlockSpec((tq, N_HEADS * HEAD_DIM), lambda i: (i, 0)),
        out_shape=jax.ShapeDtypeStruct((seq, N_HEADS * HEAD_DIM), BF16),
        scratch_shapes=[pltpu.VMEM((seq, tq), F32),
                        pltpu.VMEM((N_HEADS // HEADS_PER_UNIT, HEAD_DIM, HEADS_PER_UNIT * tq), F32),
                        pltpu.VMEM((N_HEADS // HEADS_PER_UNIT, 1, HEADS_PER_UNIT * tq), F32),
                        pltpu.VMEM((N_HEADS // HEADS_PER_UNIT, 1, HEADS_PER_UNIT * tq), F32),
                        pltpu.VMEM((2, tq, HEADS_PER_UNIT * tq), F32),
                        pltpu.SMEM((N_KV_HEADS,), F32)],
        compiler_params=_cparams(1),
        name="prompt_attention",
    )(qT, qiT, wT, ki, kb, vT)


def _sample_score_kernel(pt_ref, qi_ref, w_ref, kin_ref, *rest, n_seq, n_pages, page, t_new):
    ki_refs = rest[:n_seq * n_pages]
    sc_ref = rest[n_seq * n_pages]
    del pt_ref
    nt = (((1,), (1,)), ((), ()))
    lane = lax.broadcasted_iota(jnp.int32, (1, page), 1)
    trow = lax.broadcasted_iota(jnp.int32, (t_new, 1), 0)
    for s in range(n_seq):
        qi = qi_ref[s]
        wb = jnp.broadcast_to(w_ref[s] * IDX_SCALE, (IDX_HEADS * t_new, page))

        def head_sum(d, wb=wb):
            r = jnp.maximum(d, 0.0) * wb
            return jnp.sum(r.reshape(IDX_HEADS, t_new, page), axis=0)

        for p in range(n_pages):
            sc_ref[s, :, p * page:(p + 1) * page] = head_sum(
                jnp.dot(qi, ki_refs[s * n_pages + p][...].astype(BF16),
                        preferred_element_type=F32))
        kin = jnp.concatenate(
            [kin_ref[s], jnp.zeros((page - t_new, IDX_DIM), F32)], axis=0).astype(BF16)
        s_new = head_sum(lax.dot_general(qi, kin, nt, preferred_element_type=F32))
        sc_ref[s, :, n_pages * page:] = jnp.where(lane <= trow, s_new, -jnp.inf)


def _select_kernel(sc_ref, thr_ref, jlim_ref, *, t_new, past, topk):
    rows, total = sc_ref.shape
    kf = float(topk)
    lane = lax.broadcasted_iota(jnp.int32, (1, total), 1)
    trow = lax.broadcasted_iota(jnp.int32, (rows, 1), 0) % t_new

    def count_where(pred):
        return jnp.sum(jnp.where(pred(sc_ref[...], lane), 1.0, 0.0), axis=1, keepdims=True)

    sc_all = sc_ref[...]
    mx = jnp.max(sc_all, axis=1, keepdims=True)
    mn = jnp.min(jnp.where(sc_all == -jnp.inf, jnp.inf, sc_all), axis=1, keepdims=True)
    need = (past + trow + 1) > topk
    thr, tie = _bisect_threshold(
        lambda t: count_where(lambda s, kp: s >= t), mn, mx, need, kf)
    thr = jnp.where(need, thr, -jnp.inf)

    any_tie = jnp.max(tie) > 0.5
    nbits = max(1, (total - 1).bit_length())

    def tie_search(_):
        c_gt = count_where(lambda s, kp: s > thr)
        r = kf - c_gt

        def body(b, j):
            cand = j + jnp.left_shift(jnp.int32(1), nbits - 1 - b)
            cnt = count_where(lambda s, kp: jnp.logical_and(s == thr, kp < cand))
            return jnp.where(cnt < r, cand, j)
        j = lax.fori_loop(0, nbits, body, jnp.zeros((rows, 1), jnp.int32))
        return jnp.where(tie > 0.5, j, jnp.int32(total))

    jlim = lax.cond(any_tie, tie_search,
                    lambda _: jnp.full((rows, 1), total, jnp.int32), 0)
    thr_ref[...] = thr
    jlim_ref[...] = jnp.where(need, jlim, -1)


def _sample_attn_kernel(pt_ref, sc_ref, thr_ref, jlim_ref, q_ref, kn_ref, vn_ref, *rest,
                        n_pages, page, t_new):
    k_refs = rest[:n_pages]
    v_refs = rest[n_pages:2 * n_pages]
    o_ref = rest[2 * n_pages]
    s_ref = rest[2 * n_pages + 1]
    del pt_ref
    nchunk = n_pages + 1
    nt = (((1,), (1,)), ((), ()))

    def pad_rows(x, rows):
        return jnp.concatenate(
            [x, jnp.zeros((rows - x.shape[0], x.shape[1]), x.dtype)], axis=0).astype(BF16)

    lane = lax.broadcasted_iota(jnp.int32, (1, page), 1)
    thr = thr_ref[...]
    jlim = jlim_ref[...]

    q = q_ref[...]
    rows_g = GROUP * t_new

    def head_rows(ref, g, n_keys):
        return ref[pl.ds(g, n_keys, stride=N_KV_HEADS), :]

    def logits_chunk(c, k_of):
        s = sc_ref[:, c * page:(c + 1) * page]
        kpos = c * page + lane
        sel = jnp.logical_or(s > thr, jnp.logical_and(s == thr, kpos <= jlim))
        bias = jnp.where(sel, 0.0, MASKED)
        bias4 = jnp.concatenate([bias] * GROUP, axis=0)
        for g in range(N_KV_HEADS):
            st = lax.dot_general(q[g * rows_g:(g + 1) * rows_g, :], k_of(g), nt,
                                 preferred_element_type=F32)
            s_ref[g * rows_g:(g + 1) * rows_g, c * page:(c + 1) * page] = st + bias4

    for p in range(n_pages):
        logits_chunk(p, lambda g, p=p: head_rows(k_refs[p], g, page).astype(BF16))
    logits_chunk(n_pages, lambda g: pad_rows(head_rows(kn_ref, g, t_new), page))

    s_all = s_ref[...]
    m = jnp.max(s_all, axis=1, keepdims=True)
    pr = jnp.exp2(s_all - m)
    l = jnp.sum(pr, axis=1, keepdims=True)
    s_ref[...] = pr

    accs = [jnp.zeros((rows_g, HEAD_DIM), F32) for _ in range(N_KV_HEADS)]
    for c in range(nchunk):
        for g in range(N_KV_HEADS):
            vc = (head_rows(v_refs[c], g, page).astype(BF16) if c < n_pages
                  else pad_rows(head_rows(vn_ref, g, t_new), page))
            pc = s_ref[g * rows_g:(g + 1) * rows_g, c * page:(c + 1) * page].astype(BF16)
            accs[g] = accs[g] + jnp.dot(pc, vc, preferred_element_type=F32)

    inv = 1.0 / l
    for g in range(N_KV_HEADS):
        og = accs[g] * inv[g * rows_g:(g + 1) * rows_g, :]
        for j in range(GROUP):
            h = g * GROUP + j
            o_ref[:, h * HEAD_DIM:(h + 1) * HEAD_DIM] = (
                og[j * t_new:(j + 1) * t_new, :].astype(o_ref.dtype))


def _sample_attention(layer, page_table, qi_s, w_s, q_s, ki_new, k_new, v_new,
                      cache_ki, cache_k, cache_v):
    db, n_pages = page_table.shape
    page = cache_k.shape[2]
    t_new = ki_new.shape[1]
    past = n_pages * page
    topk = min(TOPK_MAX, (past + t_new) // 4)
    rows = N_HEADS * t_new
    kv_rows = page * N_KV_HEADS
    new_rows = t_new * N_KV_HEADS
    total = (n_pages + 1) * page

    def per_seq(shape):
        return pl.BlockSpec((None,) + shape, lambda b, pt: (b, 0, 0))

    def paged(shape, p):
        return pl.BlockSpec((None, None) + shape,
                            lambda b, pt, p=p: (layer, pt[b, p], 0, 0))

    ck = cache_k.reshape(cache_k.shape[0], cache_k.shape[1], kv_rows, HEAD_DIM)
    cv = cache_v.reshape(cache_v.shape[0], cache_v.shape[1], kv_rows, HEAD_DIM)
    ckit = jnp.swapaxes(cache_ki, 2, 3)

    n_seq = SCORE_SEQS if db % SCORE_SEQS == 0 else 1

    def seq_group(shape):
        return pl.BlockSpec((n_seq,) + shape, lambda b, pt: (b, 0, 0))

    def paged_of(shape, s, p):
        return pl.BlockSpec((None, None) + shape,
                            lambda b, pt, s=s, p=p: (layer, pt[b * n_seq + s, p], 0, 0))

    scores = pl.pallas_call(
        functools.partial(_sample_score_kernel, n_seq=n_seq, n_pages=n_pages, page=page,
                          t_new=t_new),
        grid_spec=pltpu.PrefetchScalarGridSpec(
            num_scalar_prefetch=1,
            grid=(db // n_seq,),
            in_specs=[seq_group((rows, IDX_DIM)), seq_group((rows, 1)),
                      seq_group((t_new, IDX_DIM))]
            + [paged_of((IDX_DIM, page), s, p) for s in range(n_seq) for p in range(n_pages)],
            out_specs=seq_group((t_new, total)),
        ),
        out_shape=jax.ShapeDtypeStruct((db, t_new, total), F32),
        compiler_params=_cparams(1),
        name="sample_scores",
    )(page_table, qi_s, w_s, ki_new, *([ckit] * (n_seq * n_pages)))

    thr, jlim = pl.pallas_call(
        functools.partial(_select_kernel, t_new=t_new, past=past, topk=topk),
        out_shape=[jax.ShapeDtypeStruct((db * t_new, 1), F32),
                   jax.ShapeDtypeStruct((db * t_new, 1), jnp.int32)],
        compiler_params=pltpu.CompilerParams(vmem_limit_bytes=VMEM_LIMIT),
        name="sample_select",
    )(scores.reshape(db * t_new, total))

    return pl.pallas_call(
        functools.partial(_sample_attn_kernel, n_pages=n_pages, page=page, t_new=t_new),
        grid_spec=pltpu.PrefetchScalarGridSpec(
            num_scalar_prefetch=1,
            grid=(db,),
            in_specs=[per_seq((t_new, total)), per_seq((t_new, 1)), per_seq((t_new, 1)),
                      per_seq((rows, HEAD_DIM)), per_seq((new_rows, HEAD_DIM)),
                      per_seq((new_rows, HEAD_DIM))]
            + [paged((kv_rows, HEAD_DIM), p) for p in range(n_pages)]
            + [paged((kv_rows, HEAD_DIM), p) for p in range(n_pages)],
            out_specs=pl.BlockSpec((None, t_new, N_HEADS * HEAD_DIM), lambda b, pt: (b, 0, 0)),
            scratch_shapes=[pltpu.VMEM((rows, total), F32)],
        ),
        out_shape=jax.ShapeDtypeStruct((db, t_new, N_HEADS * HEAD_DIM), F32),
        compiler_params=_cparams(1),
        name="sample_attention",
    )(page_table, scores, thr.reshape(db, t_new, 1), jlim.reshape(db, t_new, 1), q_s,
      k_new, v_new, *([ck] * n_pages), *([cv] * n_pages))


def _conv_kernel(u_ref, halo_ref, cb_ref, w_ref, o_ref, *, seq_len, tm):
    i = pl.program_id(0)
    u = u_ref[...]
    halo = halo_ref[...]
    r = lax.broadcasted_iota(jnp.int32, (tm, 1), 0)
    if seq_len >= tm:
        t = r
        halo = jnp.where(i > 0, halo, 0.0)
        h7 = halo[SUBLANES - 1:SUBLANES, :]
        h6 = halo[SUBLANES - 2:SUBLANES - 1, :]
        pre1 = jnp.broadcast_to(h7, u.shape)
        pre2 = jnp.where(r == 0, h6, h7)
    else:
        t = r % seq_len
        pre2 = halo
        pre1 = pltpu.roll(halo, tm - 1, 0)
    u1 = jnp.where(t >= 1, pltpu.roll(u, 1, 0), pre1)
    u2 = jnp.where(t >= 2, pltpu.roll(u, 2, 0), pre2)
    y = w_ref[0:1, :] * u2 + w_ref[1:2, :] * u1 + w_ref[2:3, :] * u
    o_ref[...] = (cb_ref[...] * y).astype(o_ref.dtype)


def _conv_branch(u, cb, w_conv, history, seq_len):
    t, c = u.shape
    tm = _row_tile(t)
    if history is None:
        halo_arr = u
        halo_spec = pl.BlockSpec(
            (SUBLANES, c), lambda i: (jnp.maximum(i * (tm // SUBLANES) - 1, 0), 0))
    else:
        halo_arr = history
        halo_spec = pl.BlockSpec((tm, c), lambda i: (i, 0))
    return pl.pallas_call(
        functools.partial(_conv_kernel, seq_len=seq_len, tm=tm),
        grid=(t // tm,),
        in_specs=[pl.BlockSpec((tm, c), lambda i: (i, 0)), halo_spec,
                  pl.BlockSpec((tm, c), lambda i: (i, 0)),
                  pl.BlockSpec((CONV_WIDTH, c), lambda i: (0, 0))],
        out_specs=pl.BlockSpec((tm, c), lambda i: (i, 0)),
        out_shape=jax.ShapeDtypeStruct((t, c), BF16),
        compiler_params=_cparams(1),
        name="short_conv",
    )(u, halo_arr, cb, w_conv)


def _merge_kernel(a_ref, c_ref, x_ref, wa_ref, wc_ref, wga_ref, wgb_ref, o_ref):
    x = x_ref[...]
    a = jnp.dot(a_ref[...], wa_ref[...].astype(BF16), preferred_element_type=F32)
    ga = jax.nn.sigmoid(jnp.dot(x, wga_ref[...], preferred_element_type=F32))
    c = jnp.dot(c_ref[...], wc_ref[...].astype(BF16), preferred_element_type=F32)
    gb = jax.nn.sigmoid(jnp.dot(x, wgb_ref[...], preferred_element_type=F32))
    o_ref[...] = (ga * a + gb * c).astype(o_ref.dtype)


def _merge(a, cpre, xn, w_ao, w_co, layer, w_gates, tn=512):
    t, d = a.shape[0], w_ao.shape[2]
    tm = _row_tile(t)
    nj = d // tn
    return pl.pallas_call(
        _merge_kernel,
        grid=(t // tm, nj),
        in_specs=[pl.BlockSpec((tm, a.shape[1]), lambda i, j: (i, 0)),
                  pl.BlockSpec((tm, cpre.shape[1]), lambda i, j: (i, 0)),
                  pl.BlockSpec((tm, xn.shape[1]), lambda i, j: (i, 0)),
                  _layer_block(layer, (w_ao.shape[1], tn), lambda i, j: (0, j)),
                  _layer_block(layer, (w_co.shape[1], tn), lambda i, j: (0, j)),
                  pl.BlockSpec((xn.shape[1], tn), lambda i, j: (0, j)),
                  pl.BlockSpec((xn.shape[1], tn), lambda i, j: (0, j + nj))],
        out_specs=pl.BlockSpec((tm, tn), lambda i, j: (i, j)),
        out_shape=jax.ShapeDtypeStruct((t, d), BF16),
        compiler_params=_cparams(2),
        name="gated_merge",
    )(a, cpre, xn, w_ao, w_co, w_gates, w_gates)


def _out_proj_kernel(m_ref, w_ref, x_ref, g_ref, x1_ref, xn_ref):
    h = jnp.dot(m_ref[...], w_ref[...].astype(BF16), preferred_element_type=F32)
    x1 = x_ref[...] + h
    x1_ref[...] = x1
    xn_ref[...] = _rms_body(x1, g_ref[...]).astype(xn_ref.dtype)


def _out_proj(merged, w_o, layer, x, g):
    t, d = x.shape
    tm = _row_tile(t)
    return pl.pallas_call(
        _out_proj_kernel,
        grid=(t // tm,),
        in_specs=[pl.BlockSpec((tm, d), lambda i: (i, 0)),
                  _layer_block(layer, (d, d), lambda i: (0, 0), pipeline_mode=pl.Buffered(1)),
                  pl.BlockSpec((tm, d), lambda i: (i, 0)),
                  pl.BlockSpec((1, d), lambda i: (0, 0))],
        out_specs=[pl.BlockSpec((tm, d), lambda i: (i, 0)),
                   pl.BlockSpec((tm, d), lambda i: (i, 0))],
        out_shape=[jax.ShapeDtypeStruct((t, d), F32), jax.ShapeDtypeStruct((t, d), BF16)],
        compiler_params=_cparams(1),
        name="out_proj",
    )(merged, w_o, x, g.reshape(1, d))


def _ffn_in_kernel(x_ref, wg_ref, wu_ref, o_ref):
    x = x_ref[...]
    g = jnp.dot(x, wg_ref[...].astype(BF16), preferred_element_type=F32)
    u = jnp.dot(x, wu_ref[...].astype(BF16), preferred_element_type=F32)
    o_ref[...] = (jax.nn.silu(g) * u).astype(o_ref.dtype)


def _ffn_in(xn, w_fi, layer, tn=512):
    t, d = xn.shape
    d_ff = w_fi.shape[2] // 2
    tm = _row_tile(t, BIG_ROW_TILE)
    nj = d_ff // tn
    return pl.pallas_call(
        _ffn_in_kernel,
        grid=(t // tm, nj),
        in_specs=[pl.BlockSpec((tm, d), lambda i, j: (i, 0)),
                  _layer_block(layer, (d, tn), lambda i, j: (0, j)),
                  _layer_block(layer, (d, tn), lambda i, j: (0, j + nj))],
        out_specs=pl.BlockSpec((tm, tn), lambda i, j: (i, j)),
        out_shape=jax.ShapeDtypeStruct((t, d_ff), BF16),
        compiler_params=_cparams(2),
        name="ffn_in",
    )(xn, w_fi, w_fi)


def _ffn_out_kernel(a_ref, w_ref, x_ref, o_ref):
    o_ref[...] = x_ref[...] + jnp.dot(a_ref[...], w_ref[...], preferred_element_type=F32)


def _ffn_out(act, w_fo, layer, x1, tn=512):
    t, d = x1.shape
    d_ff = act.shape[1]
    tm = _row_tile(t, BIG_ROW_TILE)
    return pl.pallas_call(
        _ffn_out_kernel,
        grid=(t // tm, d // tn),
        in_specs=[pl.BlockSpec((tm, d_ff), lambda i, j: (i, 0)),
                  _layer_block(layer, (d_ff, tn), lambda i, j: (0, j)),
                  pl.BlockSpec((tm, tn), lambda i, j: (i, j))],
        out_specs=pl.BlockSpec((tm, tn), lambda i, j: (i, j)),
        out_shape=jax.ShapeDtypeStruct((t, d), F32),
        compiler_params=_cparams(2),
        name="ffn_out",
    )(act, w_fo, x1)


def _rope_tables(pos, dim, reps):
    inv = ROPE_THETA ** (-jnp.arange(0, dim, 2, dtype=F32) / dim)
    ang = pos.astype(F32)[:, None] * inv[None, :]
    cos, sin = jnp.cos(ang), jnp.sin(ang)
    return (jnp.tile(jnp.concatenate([cos, cos], axis=1), (1, reps)),
            jnp.tile(jnp.concatenate([-sin, sin], axis=1), (1, reps)))


def _split_w_in(w, d_model):
    attn, kv, conv = N_HEADS * HEAD_DIM, N_KV_HEADS * HEAD_DIM, d_model // 2
    sizes = (attn, kv, kv, IDX_HEADS * IDX_DIM, IDX_DIM, IDX_HEADS, conv, conv, conv,
             d_model, d_model)
    parts, off = [], 0
    for n in sizes:
        parts.append(w[:, off:off + n].astype(BF16))
        off += n
    wq, wk, wv, wqi, wki, wwi, wcb, wcc, wch, wga, wgb = parts
    pad = jnp.zeros((w.shape[0], LANES - IDX_DIM - IDX_HEADS), BF16)
    return dict(q=wq, k=wk, v=wv, qi=wqi, kiwi=jnp.concatenate([wki, wwi, pad], axis=1),
                cb=wcb, cc=wcc, ch=wch, gates=jnp.concatenate([wga, wgb], axis=1))


def _layer(x, wl, tabs, attn_fn, history, seq_len, attn_cols):
    cos128, sin128, cos64, sin64, cos_kw, sin_kw = tabs
    xn = _rmsnorm(x, wl["norm_mix"], BF16)
    rope128 = functools.partial(_proj_rope_kernel, rope=_rope128)
    rope64 = functools.partial(_proj_rope_kernel, rope=_rope64)
    qlay = "cols" if attn_cols else "rows"
    (q,) = _proj(functools.partial(rope128, scale=Q_SCALE), xn, [wl["q"]],
                 [cos128, sin128], [BF16], 512, "proj_q", (qlay,))
    k, kb = _proj(functools.partial(rope128, scale=1.0), xn, [wl["k"]],
                  [cos128, sin128], [F32, BF16], 512, "proj_k", ("heads", "rows"))
    v, vb = _proj(_proj_plain_kernel, xn, [wl["v"]], [], [F32, BF16], 512, "proj_v",
                  ("heads", qlay))
    (qi,) = _proj(functools.partial(rope64, scale=1.0), xn, [wl["qi"]],
                  [cos64, sin64], [BF16], 512, "proj_qi", (qlay,))
    (kiwi,) = _proj(functools.partial(rope64, scale=1.0), xn, [wl["kiwi"]],
                    [cos_kw, sin_kw], [F32], LANES, "proj_kiwi", ("rows",))
    cb, u = _proj(_proj_conv_kernel, xn, [wl["cb"], wl["cc"], wl["ch"]], [],
                  [F32, F32], 512, "proj_conv")
    ki = kiwi[:, :IDX_DIM]
    wi = kiwi[:, IDX_DIM:IDX_DIM + IDX_HEADS]

    a = attn_fn(q=q, k=k, kb=kb, v=v, vb=vb, qi=qi, ki=ki, wi=wi)
    cpre = _conv_branch(u, cb, wl["w_conv"], history, seq_len)
    l = wl["layer"]
    merged = _merge(a, cpre, xn, wl["w_attn_out"], wl["w_conv_out"], l, wl["gates"])
    x1, xn2 = _out_proj(merged, wl["w_o"], l, x, wl["norm_ffn"])
    act = _ffn_in(xn2, wl["w_ffn_in"], l)
    x2 = _ffn_out(act, wl["w_ffn_out"], l, x1)
    return x2, k, v, ki, u


def kernel(x_prompt, x_sample, cache_k, cache_v, cache_ki, state_conv, page_table,
           norm_mix, w_in, w_conv, w_attn_out, w_conv_out, w_o, norm_ffn, w_ffn_in,
           w_ffn_out, norm_final):
    depth = w_in.shape[0]
    bp, seq, d_model = x_prompt.shape
    db, t_new, _ = x_sample.shape
    page = cache_k.shape[2]
    past = page_table.shape[1] * page
    conv_dim = d_model // 2

    pos_p = jnp.arange(seq)
    pos_s = jnp.tile(past + jnp.arange(t_new), db)

    def tables(pos):
        c128, s128 = _rope_tables(pos, HEAD_DIM, 1)
        c64, s64 = _rope_tables(pos, IDX_DIM, LANES // IDX_DIM)
        keep = jnp.arange(LANES)[None, :] < IDX_DIM
        return (c128, s128, c64, s64, jnp.where(keep, c64, 1.0), jnp.where(keep, s64, 0.0))

    tabs_p, tabs_s = tables(pos_p), tables(pos_s)

    stacked = dict(w_attn_out=w_attn_out, w_conv_out=w_conv_out, w_o=w_o,
                   w_ffn_in=w_ffn_in, w_ffn_out=w_ffn_out.astype(BF16))
    layers = []
    for l in range(depth):
        wl = _split_w_in(w_in[l], d_model)
        wl.update(layer=l, norm_mix=norm_mix[l], norm_ffn=norm_ffn[l], w_conv=w_conv[l],
                  **stacked)
        layers.append(wl)

    tq = min(256, seq)

    def prompt_attn(q, k, kb, v, vb, qi, ki, wi):
        del k, v
        return _prompt_attention(q, qi, wi.T, ki.astype(BF16), kb, vb, tq)

    def head_major(a, width):
        nh = a.shape[1] // width
        return a.reshape(db, t_new, nh, width).transpose(0, 2, 1, 3).reshape(
            db, nh * t_new, width)

    def sample_attn_for(l):
        def fn(q, k, kb, v, vb, qi, ki, wi):
            del kb, vb
            a = _sample_attention(
                l, page_table, head_major(qi, IDX_DIM), head_major(wi, 1),
                head_major(q, HEAD_DIM), ki.reshape(db, t_new, IDX_DIM),
                k.reshape(db, t_new * N_KV_HEADS, HEAD_DIM),
                v.reshape(db, t_new * N_KV_HEADS, HEAD_DIM),
                cache_ki, cache_k, cache_v)
            return a.reshape(db * t_new, N_HEADS * HEAD_DIM).astype(BF16)
        return fn

    def run(x, tabs, make_attn, history_for, seq_len, attn_cols):
        ks, vs, kis, us = [], [], [], []
        for l in range(depth):
            x, k, v, ki, u = _layer(x, layers[l], tabs, make_attn(l),
                                    history_for(l), seq_len, attn_cols)
            ks.append(k)
            vs.append(v)
            kis.append(ki)
            us.append(u)
        return _rmsnorm(x, norm_final, F32), ks, vs, kis, us

    assert bp == 1
    y_p, ks, vs, kis, us = run(x_prompt[0], tabs_p, lambda l: prompt_attn,
                               lambda l: None, seq, True)
    y_prompt = y_p[None]
    new_k_p = jnp.stack(ks).reshape(depth, bp, seq, N_KV_HEADS, HEAD_DIM)
    new_v_p = jnp.stack(vs).reshape(depth, bp, seq, N_KV_HEADS, HEAD_DIM)
    new_ki_p = jnp.stack(kis).reshape(depth, bp, seq, IDX_DIM)
    new_conv_p = jnp.stack([u[seq - (CONV_WIDTH - 1):] for u in us]).reshape(
        depth, bp, CONV_WIDTH - 1, conv_dim)

    def history_for(l):
        h = state_conv[l]
        pad = jnp.zeros((db, t_new - (CONV_WIDTH - 1), conv_dim), F32)
        return jnp.concatenate([h, pad], axis=1).reshape(db * t_new, conv_dim)

    y_s, ks, vs, kis, us = run(x_sample.reshape(db * t_new, d_model), tabs_s,
                               sample_attn_for, history_for, t_new, False)
    y_sample = y_s.reshape(db, t_new, d_model)
    new_k_s = jnp.stack(ks).reshape(depth, db, t_new, N_KV_HEADS, HEAD_DIM)
    new_v_s = jnp.stack(vs).reshape(depth, db, t_new, N_KV_HEADS, HEAD_DIM)
    new_ki_s = jnp.stack(kis).reshape(depth, db, t_new, IDX_DIM)
    new_conv_s = jnp.stack(
        [u.reshape(db, t_new, conv_dim)[:, t_new - (CONV_WIDTH - 1):] for u in us])

    return (y_prompt, y_sample, new_k_p, new_v_p, new_ki_p, new_conv_p,
            new_k_s, new_v_s, new_ki_s, new_conv_s)
```
